```python
import jax, jax.numpy as jnp
from jax import lax
import numpy as np

D_MODEL = 2048
BATCH = 1
SEQ = 8192
DEPTH = 2

N_META = 16
BLOCK = 128
PAD = BLOCK - N_META
EPS = 1e-6
NEG = -1e30

FOX_HEADS = 8
FOX_HEAD_DIM = D_MODEL // 16
FOX_WIDTH = FOX_HEADS * FOX_HEAD_DIM

CONV_CH = D_MODEL // 2
CONV_K = 3

GLA_HEADS = 4
GLA_DK = D_MODEL // 16
GLA_DV = D_MODEL // 8
GLA_RANK = 16
GLA_TAU = 16.0

D_FF = D_MODEL * 11 // 4
MLP_CONV_K = 3

SPLIT_SIZES = (FOX_WIDTH, FOX_WIDTH, FOX_WIDTH, FOX_HEADS,
               CONV_CH, CONV_CH, CONV_CH,
               GLA_HEADS * GLA_DK, GLA_HEADS * GLA_DK, GLA_HEADS * GLA_DV, GLA_HEADS * GLA_DV, GLA_RANK,
               D_MODEL, D_MODEL, D_MODEL)
N_IN = sum(SPLIT_SIZES)
SPLIT_POINTS = tuple(sum(SPLIT_SIZES[:i + 1]) for i in range(len(SPLIT_SIZES) - 1))
FOX_F_START = 3 * FOX_WIDTH

kernel_name = 'hybrid_fox_shortconv_gla_block'


def _rmsnorm(x, g):
    xf = x.astype(jnp.float32)
    y = xf * lax.rsqrt(jnp.mean(xf * xf, axis=-1, keepdims=True) + EPS)
    return (y * g.astype(jnp.float32)).astype(x.dtype)


def _causal_dwconv(x, w):
    k_width, ch = w.shape
    return lax.conv_general_dilated(x, w[:, None, :].astype(x.dtype), window_strides=(1,),
                                    padding=[(k_width - 1, 0)],
                                    dimension_numbers=('NWC', 'WIO', 'NWC'),
                                    feature_group_count=ch)


def _to_heads_padded(t, n_heads):
    b, l, w = t.shape
    t = t.reshape(b, l, n_heads, w // n_heads).transpose(0, 2, 1, 3)
    return jnp.pad(t, ((0, 0), (0, 0), (PAD, 0), (0, 0)))


def _from_heads_padded(t):
    b, h, lp, dh = t.shape
    return t[:, :, PAD:].transpose(0, 2, 1, 3).reshape(b, lp - PAD, h * dh)


def _fox_attention(q, k, v, logf):
    bn, nh, lp, dh = q.shape
    nb = lp // BLOCK
    c = jnp.cumsum(logf, axis=-1)
    pos = jnp.arange(lp)
    key_valid = pos >= PAD
    qb = jnp.moveaxis(q.reshape(bn, nh, nb, BLOCK, dh), 2, 0)
    cb = jnp.moveaxis(c.reshape(bn, nh, nb, BLOCK), 2, 0)
    scale = dh ** -0.5

    def one_block(args):
        qi, ci, i = args
        s = jnp.einsum('bhqd,bhkd->bhqk', qi, k, preferred_element_type=jnp.float32) * scale
        s = s + ci[..., None] - c[:, :, None, :]
        qpos = i * BLOCK + jnp.arange(BLOCK)
        mask = (pos[None, :] <= qpos[:, None]) & key_valid[None, :]
        p = jax.nn.softmax(jnp.where(mask, s, NEG), axis=-1)
        return jnp.einsum('bhqk,bhkd->bhqd', p.astype(v.dtype), v)

    out = lax.map(one_block, (qb, cb, jnp.arange(nb)))
    return jnp.moveaxis(out, 0, 2).reshape(bn, nh, lp, dh)


def _gla_chunked(q, k, v, logg):
    bn, nh, lp, dk = q.shape
    dv = v.shape[-1]
    nc = lp // BLOCK
    f32 = jnp.float32

    def to_chunks(t):
        return jnp.moveaxis(t.astype(f32).reshape(bn, nh, nc, BLOCK, t.shape[-1]), 2, 0)

    causal = jnp.tril(jnp.ones((BLOCK, BLOCK), dtype=bool))

    def step(state, inp):
        qc, kc, vc, gc = inp
        b = jnp.cumsum(gc, axis=2)
        diff = b[:, :, :, None, :] - b[:, :, None, :, :]
        decay = jnp.exp(jnp.where(causal[None, None, :, :, None], diff, -jnp.inf))
        att = jnp.einsum('bhtd,bhsd,bhtsd->bhts', qc, kc, decay)
        o = (jnp.einsum('bhts,bhsv->bhtv', att, vc)
             + jnp.einsum('bhtd,bhdv->bhtv', qc * jnp.exp(b), state))
        b_last = b[:, :, -1:, :]
        state = (jnp.exp(b_last[:, :, 0, :])[..., None] * state
                 + jnp.einsum('bhsd,bhsv->bhdv', kc * jnp.exp(b_last - b), vc))
        return state, o

    s0 = jnp.zeros((bn, nh, dk, dv), f32)
    _, o = lax.scan(step, s0, (to_chunks(q), to_chunks(k), to_chunks(v), to_chunks(logg)))
    return jnp.moveaxis(o, 0, 2).reshape(bn, nh, lp, dv)


def setup_inputs(seed: int = 0) -> dict:
    key = jax.random.key(seed)
    ks = jax.random.split(key, 20)

    def nrm(k, shape, scale):
        return jax.random.normal(k, shape, jnp.float32) * scale

    w_in = nrm(ks[3], (DEPTH, D_MODEL, N_IN), D_MODEL ** -0.5)
    w_in = w_in.at[:, :, FOX_F_START:FOX_F_START + FOX_HEADS].multiply(0.1)
    return {
        'x': nrm(ks[0], (BATCH, SEQ, D_MODEL), 1.0),
        'meta_tokens': nrm(ks[1], (N_META, D_MODEL), 1.0),
        'norm1_g': 1.0 + nrm(ks[2], (DEPTH, D_MODEL), 0.1),
        'w_in': w_in,
        'fox_b_f': 3.0 + nrm(ks[4], (DEPTH, FOX_HEADS), 0.1),
        'gate_b': nrm(ks[5], (DEPTH, 3 * D_MODEL), 0.01),
        'conv_w': nrm(ks[6], (DEPTH, CONV_K, CONV_CH), CONV_K ** -0.5),
        'gla_w_g2': nrm(ks[7], (DEPTH, GLA_RANK, GLA_HEADS * GLA_DK), GLA_RANK ** -0.5),
        'gla_b_g': nrm(ks[8], (DEPTH, GLA_HEADS * GLA_DK), 0.1),
        'gla_norm_g': 1.0 + nrm(ks[9], (DEPTH, GLA_HEADS * GLA_DV), 0.1),
        'w_a_o': nrm(ks[10], (DEPTH, FOX_WIDTH, D_MODEL), FOX_WIDTH ** -0.5),
        'w_b_o': nrm(ks[11], (DEPTH, CONV_CH, D_MODEL), CONV_CH ** -0.5),
        'w_c_o': nrm(ks[12], (DEPTH, GLA_HEADS * GLA_DV, D_MODEL), (GLA_HEADS * GLA_DV) ** -0.5),
        'w_o': nrm(ks[13], (DEPTH, D_MODEL, D_MODEL), D_MODEL ** -0.5),
        'norm2_g': 1.0 + nrm(ks[14], (DEPTH, D_MODEL), 0.1),
        'w_up': nrm(ks[15], (DEPTH, D_MODEL, 2 * D_FF), D_MODEL ** -0.5),
        'mlp_conv_w': nrm(ks[16], (DEPTH, MLP_CONV_K, 2 * D_FF), MLP_CONV_K ** -0.5),
        'w_down': nrm(ks[17], (DEPTH, D_FF, D_MODEL), D_FF ** -0.5),
        'final_norm_g': 1.0 + nrm(ks[18], (D_MODEL,), 0.1),
    }


def reference(x, meta_tokens, norm1_g, w_in, fox_b_f, gate_b, conv_w, gla_w_g2, gla_b_g,
              gla_norm_g, w_a_o, w_b_o, w_c_o, w_o, norm2_g, w_up, mlp_conv_w, w_down,
              final_norm_g):
    f32 = jnp.float32
    bn = x.shape[0]
    meta = jnp.broadcast_to(meta_tokens[None].astype(x.dtype), (bn, N_META, D_MODEL))
    h = jnp.concatenate([meta, x], axis=1)
    for l in range(DEPTH):
        xn = _rmsnorm(h, norm1_g[l])
        proj = xn @ w_in[l]
        (qa, ka, va, fa, sc_b, sc_c, sc_h, qc, kc, vc, rc, glr,
         g_a, g_b, g_c) = jnp.split(proj, SPLIT_POINTS, axis=-1)

        logf = jax.nn.log_sigmoid(fa.astype(f32) + fox_b_f[l].astype(f32))
        logf = jnp.pad(logf.transpose(0, 2, 1), ((0, 0), (0, 0), (PAD, 0)))
        oa = _fox_attention(_to_heads_padded(qa, FOX_HEADS), _to_heads_padded(ka, FOX_HEADS),
                            _to_heads_padded(va, FOX_HEADS), logf)
        ya = _from_heads_padded(oa) @ w_a_o[l]

        yb = (sc_b * _causal_dwconv(sc_c * sc_h, conv_w[l])) @ w_b_o[l]

        logg = jax.nn.log_sigmoid((glr @ gla_w_g2[l]).astype(f32) + gla_b_g[l].astype(f32)) / GLA_TAU
        oc = _gla_chunked(_to_heads_padded(qc * (GLA_DK ** -0.5), GLA_HEADS),
                          _to_heads_padded(kc, GLA_HEADS), _to_heads_padded(vc, GLA_HEADS),
                          _to_heads_padded(logg, GLA_HEADS))
        oc = _from_heads_padded(oc).astype(h.dtype)
        oc = _rmsnorm(oc.reshape(oc.shape[0], oc.shape[1], GLA_HEADS, GLA_DV),
                      gla_norm_g[l].reshape(GLA_HEADS, GLA_DV)).reshape(oc.shape)
        yc = (jax.nn.silu(rc) * oc) @ w_c_o[l]

        gbias = gate_b[l]
        mix = (jax.nn.sigmoid(g_a + gbias[:D_MODEL]) * ya
               + jax.nn.sigmoid(g_b + gbias[D_MODEL:2 * D_MODEL]) * yb
               + jax.nn.sigmoid(g_c + gbias[2 * D_MODEL:]) * yc)
        h = h + mix @ w_o[l]

        u = _causal_dwconv(_rmsnorm(h, norm2_g[l]) @ w_up[l], mlp_conv_w[l])
        u_gate, u_up = jnp.split(u, 2, axis=-1)
        h = h + (jax.nn.silu(u_gate) * u_up) @ w_down[l]
    return _rmsnorm(h, final_norm_g)[:, N_META:]
```

```python
import functools

import jax
import jax.numpy as jnp
from jax import lax
from jax.experimental import pallas as pl
from jax.experimental.pallas import tpu as pltpu

F32 = jnp.float32
BF16 = jnp.bfloat16

D_MODEL = 2048
N_META = 16
EPS = 1e-6
NEG = -1e30

FOX_HEADS = 8
FOX_DH = D_MODEL // 16
FOX_WIDTH = FOX_HEADS * FOX_DH
CONV_CH = D_MODEL // 2
GLA_HEADS = 4
GLA_DK = D_MODEL // 16
GLA_DV = D_MODEL // 8
GLA_RANK = 16
GLA_TAU = 16.0
GLA_CHUNK = 128
D_FF = D_MODEL * 11 // 4

COL_GA = 0
COL_GB = COL_GA + D_MODEL
COL_GC = COL_GB + D_MODEL
COL_QA = COL_GC + D_MODEL
COL_KA = COL_QA + FOX_WIDTH
COL_VA = COL_KA + FOX_WIDTH
COL_SCB = COL_VA + FOX_WIDTH
COL_SCC = COL_SCB + CONV_CH
COL_SCH = COL_SCC + CONV_CH
COL_QC = COL_SCH + CONV_CH
COL_KC = COL_QC + GLA_HEADS * GLA_DK
COL_VC = COL_KC + GLA_HEADS * GLA_DK
COL_RC = COL_VC + GLA_HEADS * GLA_DV
N_MAIN = COL_RC + GLA_HEADS * GLA_DV
N_SMALL = 128
SMALL_GLR = FOX_HEADS

ROW_TILE = 640
HALO = 16
VMEM_LIMIT = 56 * 1024 * 1024


def _params(sem, vmem=VMEM_LIMIT):
    return pltpu.CompilerParams(dimension_semantics=sem, vmem_limit_bytes=vmem)


def _sigmoid(x):
    return 1.0 / (1.0 + jnp.exp(-x))


def _log_sigmoid(x):
    return jnp.minimum(x, 0.0) - jnp.log(1.0 + jnp.exp(-jnp.abs(x)))


def _dot(a, b):
    return jnp.dot(a, b, preferred_element_type=F32)


def _dot_nt(a, b):
    return lax.dot_general(a, b, (((1,), (1,)), ((), ())), preferred_element_type=F32)


def _split3(x):
    hi = x.astype(BF16)
    r1 = x - hi.astype(F32)
    mid = r1.astype(BF16)
    lo = (r1 - mid.astype(F32)).astype(BF16)
    return hi, mid, lo


def _rmsnorm_kernel(h_ref, g_ref, o_ref):
    x = h_ref[...]
    y = x * lax.rsqrt(jnp.mean(x * x, axis=-1, keepdims=True) + EPS)
    o_ref[...] = (y * g_ref[...]).astype(o_ref.dtype)


def _rmsnorm(h, g, out_dtype, tm):
    lp, d = h.shape
    return pl.pallas_call(
        _rmsnorm_kernel,
        grid=(lp // tm,),
        in_specs=[pl.BlockSpec((tm, d), lambda i: (i, 0)),
                  pl.BlockSpec((1, d), lambda i: (0, 0))],
        out_specs=pl.BlockSpec((tm, d), lambda i: (i, 0)),
        out_shape=jax.ShapeDtypeStruct((lp, d), out_dtype),
        compiler_params=_params(("parallel",)),
        name="rmsnorm",
    )(h, g.reshape(1, d))


def _matmul_kernel(x_ref, w_ref, s_ref, o_ref):
    o_ref[...] = (_dot(x_ref[...], w_ref[...]) * s_ref[...]).astype(o_ref.dtype)


def _matmul(x, w, colscale, out_dtype, tm, tn, name):
    lp, k = x.shape
    n = w.shape[1]
    return pl.pallas_call(
        _matmul_kernel,
        grid=(lp // tm, n // tn),
        in_specs=[pl.BlockSpec((tm, k), lambda i, j: (i, 0)),
                  pl.BlockSpec((k, tn), lambda i, j: (0, j)),
                  pl.BlockSpec((1, tn), lambda i, j: (0, j))],
        out_specs=pl.BlockSpec((tm, tn), lambda i, j: (i, j)),
        out_shape=jax.ShapeDtypeStruct((lp, n), out_dtype),
        compiler_params=_params(("parallel", "arbitrary")),
        name=name,
    )(x, w, colscale)


def _fox_cumsum_kernel(f_ref, b_ref, c_ref):
    nblk = f_ref.shape[0]
    row = lax.broadcasted_iota(jnp.int32, (128, 128), 0)
    col = lax.broadcasted_iota(jnp.int32, (128, 128), 1)
    triu = jnp.where(row <= col, 1.0, 0.0).astype(BF16)

    def body(j, carry):
        logf = _log_sigmoid(f_ref[j] + b_ref[...])
        hi, mid, lo = _split3(logf)
        cs = _dot(hi, triu) + _dot(mid, triu) + _dot(lo, triu) + carry
        c_ref[j] = cs
        return cs[:, 127:128]

    lax.fori_loop(0, nblk, body, jnp.zeros((f_ref.shape[1], 1), F32))


def _fox_cumsum(f_blocks, b_f):
    nblk, nh, _ = f_blocks.shape
    return pl.pallas_call(
        _fox_cumsum_kernel,
        out_shape=jax.ShapeDtypeStruct((nblk, nh, 128), F32),
        name="fox_cumsum",
    )(f_blocks, b_f.reshape(nh, 1))


def _fox_kernel(q_ref, k_ref, v_ref, c_ref, o_ref, s_sc, p_sc, m_sc, l_sc, acc_sc, *, blk, sub):
    i = pl.program_id(1)
    m_sc[...] = jnp.full(m_sc.shape, NEG, F32)
    l_sc[...] = jnp.zeros(l_sc.shape, F32)
    acc_sc[...] = jnp.zeros(acc_sc.shape, F32)
    q = q_ref[...]

    def kv_block(j, masked):
        rows = pl.ds(pl.multiple_of(j * blk, blk), blk)
        s_sc[...] = _dot_nt(q, k_ref[rows, :])
        cj = c_ref[j]

        def row_tile(r, _):
            sl = pl.ds(pl.multiple_of(r * sub, sub), sub)
            s = s_sc[sl, :] - cj
            if masked:
                qpos = r * sub + lax.broadcasted_iota(jnp.int32, (sub, blk), 0)
                kpos = lax.broadcasted_iota(jnp.int32, (sub, blk), 1)
                s = jnp.where(kpos <= qpos, s, NEG)
            m_prev = m_sc[sl, :]
            m_new = jnp.maximum(m_prev, jnp.max(s, axis=-1, keepdims=True))
            alpha = jnp.exp(m_prev - m_new)
            p = jnp.exp(s - m_new)
            l_sc[sl, :] = alpha * l_sc[sl, :] + jnp.sum(p, axis=-1, keepdims=True)
            m_sc[sl, :] = m_new
            acc_sc[sl, :] = alpha * acc_sc[sl, :]
            p_sc[sl, :] = p.astype(BF16)
            return 0

        lax.fori_loop(0, blk // sub, row_tile, 0)
        acc_sc[...] += _dot(p_sc[...], v_ref[rows, :])

    def full_block(j, _):
        kv_block(j, masked=False)
        return 0

    lax.fori_loop(0, i, full_block, 0)
    kv_block(i, masked=True)
    o_ref[...] = (acc_sc[...] / l_sc[...]).astype(o_ref.dtype)


def _fox_attention(proj, c_blocks, blk, sub):
    lp = proj.shape[0]
    nb = lp // blk
    dh = FOX_DH
    kern = functools.partial(_fox_kernel, blk=blk, sub=sub)
    return pl.pallas_call(
        kern,
        grid=(FOX_HEADS, nb),
        in_specs=[pl.BlockSpec((blk, dh), lambda h, i: (i, COL_QA // dh + h)),
                  pl.BlockSpec((lp, dh), lambda h, i: (0, COL_KA // dh + h)),
                  pl.BlockSpec((lp, dh), lambda h, i: (0, COL_VA // dh + h)),
                  pl.BlockSpec((None, nb, 1, blk), lambda h, i: (h, 0, 0, 0))],
        out_specs=pl.BlockSpec((blk, dh), lambda h, i: (i, h)),
        out_shape=jax.ShapeDtypeStruct((lp, FOX_WIDTH), BF16),
        scratch_shapes=[pltpu.VMEM((blk, blk), F32),
                        pltpu.VMEM((blk, blk), BF16),
                        pltpu.VMEM((blk, 1), F32),
                        pltpu.VMEM((blk, 1), F32),
                        pltpu.VMEM((blk, dh), F32)],
        compiler_params=_params(("parallel", "arbitrary")),
        name="fox_attention",
    )(proj, proj, proj, c_blocks)


def _segment_reference_rows(b, n, sub8):
    c = b.shape[0]
    half = n // 2
    if n >= 16:
        pieces = [jnp.broadcast_to(b[m * n + half - 1:m * n + half, :], (n, b.shape[1]))
                  for m in range(c // n)]
        return pieces[0] if len(pieces) == 1 else jnp.concatenate(pieces, axis=0)
    b3 = b.reshape(c // 8, 8, b.shape[1])

    def bc(r):
        return jnp.broadcast_to(b3[:, r:r + 1, :], b3.shape).reshape(b.shape)

    out = bc(8 - n + half - 1)
    for start in range(8 - 2 * n, -1, -n):
        out = jnp.where(sub8 < start + n, bc(start + half - 1), out)
    return out


def _gla_kernel(q_ref, k_ref, v_ref, r_ref, sm_ref, wg_ref, bg_ref, ng_ref, o_ref, st_ref):
    c = GLA_CHUNK

    @pl.when(pl.program_id(1) == 0)
    def _():
        st_ref[...] = jnp.zeros(st_ref.shape, F32)

    q = q_ref[...].astype(F32)
    k = k_ref[...].astype(F32)
    v = v_ref[...]
    z = _dot(sm_ref[...].astype(BF16), wg_ref[...].astype(BF16)) + bg_ref[...]
    g = _log_sigmoid(z) * (1.0 / GLA_TAU)

    t_idx = lax.broadcasted_iota(jnp.int32, (c, c), 0)
    s_idx = lax.broadcasted_iota(jnp.int32, (c, c), 1)
    tril = jnp.where(s_idx <= t_idx, 1.0, 0.0).astype(BF16)
    hi, mid, lo = _split3(g)
    b = _dot(tril, hi) + _dot(tril, mid) + _dot(tril, lo)

    sub8 = t_idx & 7
    att = jnp.where(t_idx == s_idx, _dot_nt(q.astype(BF16), k.astype(BF16)), 0.0)
    n = c
    while n >= 2:
        upper = (t_idx & (n - 1)) >= n // 2
        ref = _segment_reference_rows(b, n, sub8)
        qs = jnp.where(upper, q * jnp.exp(jnp.minimum(b - ref, 0.0)), 0.0).astype(BF16)
        ks = jnp.where(upper, 0.0, k * jnp.exp(jnp.minimum(ref - b, 0.0))).astype(BF16)
        a = _dot_nt(qs, ks)
        if n < c:
            shift = n.bit_length() - 1
            a = jnp.where((t_idx >> shift) == (s_idx >> shift), a, 0.0)
        att = att + a
        n //= 2

    st = st_ref[...]
    o = _dot(att.astype(BF16), v) + _dot_nt((q * jnp.exp(b)).astype(BF16), st.astype(BF16))
    b_last = b[c - 1:c, :]
    ke = (k * jnp.exp(b_last - b)).astype(BF16)
    st_ref[...] = st * jnp.exp(b_last) + lax.dot_general(
        v, ke, (((0,), (0,)), ((), ())), preferred_element_type=F32)

    on = o * lax.rsqrt(jnp.mean(o * o, axis=-1, keepdims=True) + EPS) * ng_ref[...]
    r = r_ref[...].astype(F32)
    o_ref[...] = (r * _sigmoid(r) * on).astype(o_ref.dtype)


def _gla(proj, small, wg_pad, b_g, norm_g):
    lp = proj.shape[0]
    c, dk, dv = GLA_CHUNK, GLA_DK, GLA_DV
    return pl.pallas_call(
        _gla_kernel,
        grid=(GLA_HEADS, lp // c),
        in_specs=[pl.BlockSpec((c, dk), lambda h, i: (i, COL_QC // dk + h)),
                  pl.BlockSpec((c, dk), lambda h, i: (i, COL_KC // dk + h)),
                  pl.BlockSpec((c, dv), lambda h, i: (i, COL_VC // dv + h)),
                  pl.BlockSpec((c, dv), lambda h, i: (i, COL_RC // dv + h)),
                  pl.BlockSpec((c, N_SMALL), lambda h, i: (i, 0)),
                  pl.BlockSpec((None, N_SMALL, dk), lambda h, i: (h, 0, 0)),
                  pl.BlockSpec((None, 1, dk), lambda h, i: (h, 0, 0)),
                  pl.BlockSpec((None, 1, dv), lambda h, i: (h, 0, 0))],
        out_specs=pl.BlockSpec((c, dv), lambda h, i: (i, h)),
        out_shape=jax.ShapeDtypeStruct((lp, GLA_HEADS * dv), BF16),
        scratch_shapes=[pltpu.VMEM((dv, dk), F32)],
        compiler_params=_params(("parallel", "arbitrary")),
        name="gla",
    )(proj, proj, proj, proj, small, wg_pad, b_g, norm_g)


def _causal_conv3(ext, w_ref):
    m1 = pltpu.roll(ext, 1, 0)
    m2 = pltpu.roll(ext, 2, 0)
    return (ext[HALO:] * w_ref[2:3, :] + m1[HALO:] * w_ref[1:2, :] + m2[HALO:] * w_ref[0:1, :])


def _merge_kernel(oa_ref, scb_ref, scc_ref, sch_ref, scc_h_ref, sch_h_ref, oc_ref,
                  ga_ref, gb_ref, gc_ref, gbias_ref, cw_ref, wa_ref, wb_ref, wc_ref, o_ref):
    keep = (pl.program_id(0) > 0).astype(F32)
    p_halo = scc_h_ref[...].astype(F32) * sch_h_ref[...].astype(F32) * keep
    p = scc_ref[...].astype(F32) * sch_ref[...].astype(F32)
    conv = _causal_conv3(jnp.concatenate([p_halo, p], axis=0), cw_ref)
    ub = (scb_ref[...].astype(F32) * conv).astype(BF16)
    ya = _dot(oa_ref[...], wa_ref[...])
    yb = _dot(ub, wb_ref[...])
    yc = _dot(oc_ref[...], wc_ref[...])
    mix = (_sigmoid(ga_ref[...].astype(F32) + gbias_ref[0:1, :]) * ya
           + _sigmoid(gb_ref[...].astype(F32) + gbias_ref[1:2, :]) * yb
           + _sigmoid(gc_ref[...].astype(F32) + gbias_ref[2:3, :]) * yc)
    o_ref[...] = mix.astype(o_ref.dtype)


def _merge(proj, oa, oc, gate_b, conv_w, w_a_o, w_b_o, w_c_o, tm):
    lp = proj.shape[0]
    d = D_MODEL
    ch = CONV_CH
    hb = tm // HALO

    def col(c0, width):
        return lambda i: (i, c0 // width)

    def halo(c0):
        return lambda i: (jnp.maximum(i * hb - 1, 0), c0 // ch)

    def const(shape):
        return pl.BlockSpec(shape, lambda i: (0, 0), pipeline_mode=pl.Buffered(1))

    return pl.pallas_call(
        _merge_kernel,
        grid=(lp // tm,),
        in_specs=[pl.BlockSpec((tm, FOX_WIDTH), lambda i: (i, 0)),
                  pl.BlockSpec((tm, ch), col(COL_SCB, ch)),
                  pl.BlockSpec((tm, ch), col(COL_SCC, ch)),
                  pl.BlockSpec((tm, ch), col(COL_SCH, ch)),
                  pl.BlockSpec((HALO, ch), halo(COL_SCC)),
                  pl.BlockSpec((HALO, ch), halo(COL_SCH)),
                  pl.BlockSpec((tm, GLA_HEADS * GLA_DV), lambda i: (i, 0)),
                  pl.BlockSpec((tm, d), col(COL_GA, d)),
                  pl.BlockSpec((tm, d), col(COL_GB, d)),
                  pl.BlockSpec((tm, d), col(COL_GC, d)),
                  const((3, d)),
                  const((3, ch)),
                  const((FOX_WIDTH, d)),
                  const((ch, d)),
                  const((GLA_HEADS * GLA_DV, d))],
        out_specs=pl.BlockSpec((tm, d), lambda i: (i, 0)),
        out_shape=jax.ShapeDtypeStruct((lp, d), BF16),
        compiler_params=_params(("parallel",)),
        name="merge",
    )(oa, proj, proj, proj, proj, proj, oc, proj, proj, proj,
      gate_b.reshape(3, d), conv_w, w_a_o, w_b_o, w_c_o)


def _resid_matmul_kernel(x_ref, w_ref, h_ref, g_ref, ho_ref, no_ref, acc_ref):
    kk = pl.program_id(1)

    @pl.when(kk == 0)
    def _():
        acc_ref[...] = jnp.zeros(acc_ref.shape, F32)

    acc_ref[...] += _dot(x_ref[...], w_ref[...])

    @pl.when(kk == pl.num_programs(1) - 1)
    def _():
        hn = h_ref[...] + acc_ref[...]
        ho_ref[...] = hn
        y = hn * lax.rsqrt(jnp.mean(hn * hn, axis=-1, keepdims=True) + EPS)
        no_ref[...] = (y * g_ref[...]).astype(no_ref.dtype)


def _resid_matmul(x, w, h, g, norm_dtype, tm, tk, name):
    lp, k = x.shape
    d = w.shape[1]
    return pl.pallas_call(
        _resid_matmul_kernel,
        grid=(lp // tm, k // tk),
        in_specs=[pl.BlockSpec((tm, tk), lambda i, kk: (i, kk)),
                  pl.BlockSpec((tk, d), lambda i, kk: (kk, 0)),
                  pl.BlockSpec((tm, d), lambda i, kk: (i, 0)),
                  pl.BlockSpec((1, d), lambda i, kk: (0, 0))],
        out_specs=[pl.BlockSpec((tm, d), lambda i, kk: (i, 0)),
                   pl.BlockSpec((tm, d), lambda i, kk: (i, 0))],
        out_shape=[jax.ShapeDtypeStruct((lp, d), F32),
                   jax.ShapeDtypeStruct((lp, d), norm_dtype)],
        scratch_shapes=[pltpu.VMEM((tm, d), F32)],
        compiler_params=_params(("parallel", "arbitrary")),
        name=name,
    )(x, w, h, g.reshape(1, d))


def _mlp_up_kernel(x_ref, xh_ref, wg_ref, wu_ref, cg_ref, cu_ref, o_ref):
    keep = (pl.program_id(0) > 0).astype(BF16)
    xe = jnp.concatenate([xh_ref[...] * keep, x_ref[...]], axis=0)
    ug = _causal_conv3(_dot(xe, wg_ref[...]), cg_ref)
    uu = _causal_conv3(_dot(xe, wu_ref[...]), cu_ref)
    o_ref[...] = (ug * _sigmoid(ug) * uu).astype(o_ref.dtype)


def _mlp_up(xn, w_up, conv_w, tm, tn):
    lp, d = xn.shape
    nj = D_FF // tn
    hb = tm // HALO
    return pl.pallas_call(
        _mlp_up_kernel,
        grid=(lp // tm, nj),
        in_specs=[pl.BlockSpec((tm, d), lambda i, j: (i, 0)),
                  pl.BlockSpec((HALO, d), lambda i, j: (jnp.maximum(i * hb - 1, 0), 0)),
                  pl.BlockSpec((d, tn), lambda i, j: (0, j)),
                  pl.BlockSpec((d, tn), lambda i, j: (0, nj + j)),
                  pl.BlockSpec((3, tn), lambda i, j: (0, j)),
                  pl.BlockSpec((3, tn), lambda i, j: (0, nj + j))],
        out_specs=pl.BlockSpec((tm, tn), lambda i, j: (i, j)),
        out_shape=jax.ShapeDtypeStruct((lp, D_FF), BF16),
        compiler_params=_params(("parallel", "arbitrary")),
        name="mlp_up",
    )(xn, xn, w_up, w_up, conv_w, conv_w)


def _reorder_w_in(w):
    f0 = 3 * FOX_WIDTH
    f1 = f0 + FOX_HEADS
    r0 = f1 + 3 * CONV_CH + 2 * GLA_HEADS * GLA_DK + 2 * GLA_HEADS * GLA_DV
    r1 = r0 + GLA_RANK
    main = jnp.concatenate([w[:, r1:], w[:, :f0], w[:, f1:r0]], axis=1).astype(BF16)
    pad = jnp.zeros((w.shape[0], N_SMALL - FOX_HEADS - GLA_RANK), w.dtype)
    small = jnp.concatenate([w[:, f0:f1], w[:, r0:r1], pad], axis=1).astype(BF16)
    return main, small


def _main_colscale():
    s = jnp.ones((1, N_MAIN), F32)
    s = s.at[:, COL_QA:COL_QA + FOX_WIDTH].set(FOX_DH ** -0.5)
    s = s.at[:, COL_QC:COL_QC + GLA_HEADS * GLA_DK].set(GLA_DK ** -0.5)
    return s


def kernel(x, meta_tokens, norm1_g, w_in, fox_b_f, gate_b, conv_w, gla_w_g2, gla_b_g, gla_norm_g,
           w_a_o, w_b_o, w_c_o, w_o, norm2_g, w_up, mlp_conv_w, w_down, final_norm_g):
    assert x.shape[0] == 1 and x.shape[2] == D_MODEL
    depth = w_in.shape[0]
    seq = x.shape[1]
    l_real = N_META + seq
    lp = -(-l_real // ROW_TILE) * ROW_TILE
    tm = ROW_TILE
    blk = ROW_TILE

    h = jnp.concatenate([meta_tokens.astype(x.dtype), x[0],
                         jnp.zeros((lp - l_real, D_MODEL), x.dtype)], axis=0)
    colscale = _main_colscale()
    ones_small = jnp.ones((1, N_SMALL), F32)
    xn = _rmsnorm(h, norm1_g[0], BF16, tm)

    out = None
    for l in range(depth):
        w_main, w_small = _reorder_w_in(w_in[l])
        proj = _matmul(xn, w_main, colscale, BF16, tm, 512, "in_proj")
        small = _matmul(xn, w_small, ones_small, F32, tm, N_SMALL, "in_proj_small")

        f_blocks = small[:, :FOX_HEADS].T.reshape(FOX_HEADS, lp // 128, 128).transpose(1, 0, 2)
        c = _fox_cumsum(f_blocks, fox_b_f[l])
        c_blocks = c.transpose(1, 0, 2).reshape(FOX_HEADS, lp // blk, 1, blk)
        oa = _fox_attention(proj, c_blocks, blk, 64)

        wg = gla_w_g2[l].reshape(GLA_RANK, GLA_HEADS, GLA_DK).transpose(1, 0, 2)
        wg_pad = jnp.zeros((GLA_HEADS, N_SMALL, GLA_DK), F32)
        wg_pad = wg_pad.at[:, SMALL_GLR:SMALL_GLR + GLA_RANK, :].set(wg)
        oc = _gla(proj, small, wg_pad, gla_b_g[l].reshape(GLA_HEADS, 1, GLA_DK),
                  gla_norm_g[l].reshape(GLA_HEADS, 1, GLA_DV))

        mix = _merge(proj, oa, oc, gate_b[l], conv_w[l], w_a_o[l].astype(BF16),
                     w_b_o[l].astype(BF16), w_c_o[l].astype(BF16), 320)
        h, xn2 = _resid_matmul(mix, w_o[l].astype(BF16), h, norm2_g[l], BF16, 320, D_MODEL, "out_proj")

        act = _mlp_up(xn2, w_up[l].astype(BF16), mlp_conv_w[l], tm, 512)
        last = l == depth - 1
        g_next = final_norm_g if last else norm1_g[l + 1]
        h, xn = _resid_matmul(act, w_down[l].astype(BF16), h, g_next,
                              F32 if last else BF16, tm, 512, "mlp_down")
        out = xn
    return out[N_META:l_real][None]
```

```python
import functools

import jax
import jax.numpy as jnp
from jax import lax
from jax.experimental import pallas as pl
from jax.experimental.pallas import tpu as pltpu

F32 = jnp.float32
BF16 = jnp.bfloat16

D_MODEL = 2048
N_META = 16
EPS = 1e-6
NEG = -1e30
LOG2_E = 1.4426950408889634

FOX_HEADS = 8
FOX_DH = D_MODEL // 16
FOX_WIDTH = FOX_HEADS * FOX_DH
CONV_CH = D_MODEL // 2
GLA_HEADS = 4
GLA_DK = D_MODEL // 16
GLA_DV = D_MODEL // 8
GLA_RANK = 16
GLA_TAU = 16.0
GLA_CHUNK = 128
D_FF = D_MODEL * 11 // 4

COL_GA = 0
COL_GB = COL_GA + D_MODEL
COL_GC = COL_GB + D_MODEL
COL_QA = COL_GC + D_MODEL
COL_KA = COL_QA + FOX_WIDTH
COL_VA = COL_KA + FOX_WIDTH
COL_SCB = COL_VA + FOX_WIDTH
COL_SCC = COL_SCB + CONV_CH
COL_SCH = COL_SCC + CONV_CH
COL_QC = COL_SCH + CONV_CH
COL_KC = COL_QC + GLA_HEADS * GLA_DK
COL_VC = COL_KC + GLA_HEADS * GLA_DK
COL_RC = COL_VC + GLA_HEADS * GLA_DV
N_MAIN = COL_RC + GLA_HEADS * GLA_DV
N_SMALL = 128
SMALL_GLR = FOX_HEADS

ROW_TILE = 640
HALO = 16
VMEM_LIMIT = 56 * 1024 * 1024


def _row_tile(lp, cap):
    return max(t for t in range(128, min(cap, lp) + 1, 128) if lp % t == 0)


def _params(sem, vmem=VMEM_LIMIT):
    return pltpu.CompilerParams(dimension_semantics=sem, vmem_limit_bytes=vmem)


def _sigmoid(x):
    return 1.0 / (1.0 + jnp.exp(-x))


def _log_sigmoid(x):
    return jnp.minimum(x, 0.0) - jnp.log(1.0 + jnp.exp(-jnp.abs(x)))


def _dot(a, b):
    return jnp.dot(a, b, preferred_element_type=F32)


def _dot_nt(a, b):
    return lax.dot_general(a, b, (((1,), (1,)), ((), ())), preferred_element_type=F32)


def _split3(x):
    hi = x.astype(BF16)
    r1 = x - hi.astype(F32)
    mid = r1.astype(BF16)
    lo = (r1 - mid.astype(F32)).astype(BF16)
    return hi, mid, lo


def _rmsnorm_kernel(h_ref, g_ref, o_ref):
    x = h_ref[...]
    y = x * lax.rsqrt(jnp.mean(x * x, axis=-1, keepdims=True) + EPS)
    o_ref[...] = (y * g_ref[...]).astype(o_ref.dtype)


def _rmsnorm(h, g, out_dtype, tm):
    lp, d = h.shape
    return pl.pallas_call(
        _rmsnorm_kernel,
        grid=(lp // tm,),
        in_specs=[pl.BlockSpec((tm, d), lambda i: (i, 0)),
                  pl.BlockSpec((1, d), lambda i: (0, 0))],
        out_specs=pl.BlockSpec((tm, d), lambda i: (i, 0)),
        out_shape=jax.ShapeDtypeStruct((lp, d), out_dtype),
        compiler_params=_params(("parallel",)),
        name="rmsnorm",
    )(h, g.reshape(1, d))


def _matmul_kernel(x_ref, w_ref, s_ref, o_ref):
    o_ref[...] = (_dot(x_ref[...], w_ref[...]) * s_ref[...]).astype(o_ref.dtype)


def _matmul(x, w, colscale, out_dtype, tm, tn, name):
    lp, k = x.shape
    n = w.shape[1]
    return pl.pallas_call(
        _matmul_kernel,
        grid=(lp // tm, n // tn),
        in_specs=[pl.BlockSpec((tm, k), lambda i, j: (i, 0)),
                  pl.BlockSpec((k, tn), lambda i, j: (0, j)),
                  pl.BlockSpec((1, tn), lambda i, j: (0, j))],
        out_specs=pl.BlockSpec((tm, tn), lambda i, j: (i, j)),
        out_shape=jax.ShapeDtypeStruct((lp, n), out_dtype),
        compiler_params=_params(("parallel", "arbitrary")),
        name=name,
    )(x, w, colscale)


def _fox_cumsum_kernel(f_ref, b_ref, c_ref):
    nblk = f_ref.shape[0]
    row = lax.broadcasted_iota(jnp.int32, (128, 128), 0)
    col = lax.broadcasted_iota(jnp.int32, (128, 128), 1)
    triu = jnp.where(row <= col, 1.0, 0.0).astype(BF16)

    def body(j, carry):
        logf = _log_sigmoid(f_ref[j] + b_ref[...]) * LOG2_E
        hi, mid, lo = _split3(logf)
        cs = _dot(hi, triu) + _dot(mid, triu) + _dot(lo, triu) + carry
        c_ref[j] = cs
        return cs[:, 127:128]

    lax.fori_loop(0, nblk, body, jnp.zeros((f_ref.shape[1], 1), F32))


def _fox_cumsum(f_blocks, b_f):
    nblk, nh, _ = f_blocks.shape
    return pl.pallas_call(
        _fox_cumsum_kernel,
        out_shape=jax.ShapeDtypeStruct((nblk, nh, 128), F32),
        name="fox_cumsum",
    )(f_blocks, b_f.reshape(nh, 1))


def _fox_kernel(q_ref, k_ref, v_ref, c_ref, o_ref, vx_sc, s_sc, p_sc, m_sc, acc_sc,
                *, blk, part, sub):
    i = pl.program_id(1)
    dh = q_ref.shape[1]

    @pl.when(i == 0)
    def _():
        vx_sc[:, :dh] = v_ref[...]
        vx_sc[:, dh:] = jnp.ones((vx_sc.shape[0], dh), BF16)

    m_sc[...] = jnp.full(m_sc.shape, NEG, F32)
    acc_sc[...] = jnp.zeros(acc_sc.shape, F32)

    def rows_of(j):
        return pl.ds(pl.multiple_of(j * blk, blk), blk)

    def logits(j, slot):
        kj = k_ref[rows_of(j), :]
        for a in range(0, blk, part):
            s_sc[slot, a:a + part, :] = _dot_nt(q_ref[a:a + part, :], kj)

    def kv_block(j, slot, masked, prefetch):
        if prefetch:
            logits(j + 1, 1 - slot)
        vj = vx_sc[rows_of(j), :]
        cj = c_ref[j]
        for a in range(0, blk, part):
            for r in range(a, a + part, sub):
                s = s_sc[slot, r:r + sub, :] - cj
                if masked:
                    qpos = r + lax.broadcasted_iota(jnp.int32, (sub, blk), 0)
                    kpos = lax.broadcasted_iota(jnp.int32, (sub, blk), 1)
                    s = jnp.where(kpos <= qpos, s, NEG)
                m_prev = m_sc[r:r + sub, :]
                m_new = jnp.maximum(m_prev, jnp.max(s, axis=-1, keepdims=True))
                p = jnp.exp2(s - pltpu.repeat(m_new, blk // 128, axis=1))
                p_sc[r:r + sub, :] = p.astype(BF16)
                alpha = jnp.exp2(m_prev - m_new)
                m_sc[r:r + sub, :] = m_new
                acc_sc[r:r + sub, :] = acc_sc[r:r + sub, :] * pltpu.repeat(alpha, 2, axis=1)
            acc_sc[a:a + part, :] += _dot(p_sc[a:a + part, :], vj)

    def block_pair(t, _):
        kv_block(2 * t, 0, masked=False, prefetch=True)
        kv_block(2 * t + 1, 1, masked=False, prefetch=True)
        return 0

    logits(0, 0)
    lax.fori_loop(0, i // 2, block_pair, 0)

    @pl.when(i % 2 == 1)
    def _():
        kv_block(i - 1, 0, masked=False, prefetch=True)
        kv_block(i, 1, masked=True, prefetch=False)

    @pl.when(i % 2 == 0)
    def _():
        kv_block(i, 0, masked=True, prefetch=False)

    o_ref[...] = (acc_sc[:, :dh] / acc_sc[:, dh:]).astype(o_ref.dtype)


def _fox_attention(proj, c_blocks, blk, part, sub):
    lp = proj.shape[0]
    nb = lp // blk
    dh = FOX_DH
    kern = functools.partial(_fox_kernel, blk=blk, part=part, sub=sub)
    return pl.pallas_call(
        kern,
        grid=(FOX_HEADS, nb),
        in_specs=[pl.BlockSpec((blk, dh), lambda h, i: (i, COL_QA // dh + h)),
                  pl.BlockSpec((lp, dh), lambda h, i: (0, COL_KA // dh + h)),
                  pl.BlockSpec((lp, dh), lambda h, i: (0, COL_VA // dh + h)),
                  pl.BlockSpec((None, nb, 1, blk), lambda h, i: (h, 0, 0, 0))],
        out_specs=pl.BlockSpec((blk, dh), lambda h, i: (i, h)),
        out_shape=jax.ShapeDtypeStruct((lp, FOX_WIDTH), BF16),
        scratch_shapes=[pltpu.VMEM((lp, 2 * dh), BF16),
                        pltpu.VMEM((2, blk, blk), F32),
                        pltpu.VMEM((blk, blk), BF16),
                        pltpu.VMEM((blk, 128), F32),
                        pltpu.VMEM((blk, 2 * dh), F32)],
        compiler_params=_params(("arbitrary", "arbitrary")),
        name="fox_attention",
    )(proj, proj, proj, c_blocks)


def _segment_reference_rows(b, n, sub8):
    c = b.shape[0]
    half = n // 2
    if n >= 16:
        pieces = [jnp.broadcast_to(b[m * n + half - 1:m * n + half, :], (n, b.shape[1]))
                  for m in range(c // n)]
        return pieces[0] if len(pieces) == 1 else jnp.concatenate(pieces, axis=0)
    b3 = b.reshape(c // 8, 8, b.shape[1])

    def bc(r):
        return jnp.broadcast_to(b3[:, r:r + 1, :], b3.shape).reshape(b.shape)

    out = bc(8 - n + half - 1)
    for start in range(8 - 2 * n, -1, -n):
        out = jnp.where(sub8 < start + n, bc(start + half - 1), out)
    return out


def _gla_kernel(q_ref, k_ref, v_ref, r_ref, sm_ref, wg_ref, bg_ref, ng_ref, o_ref, st_ref):
    c = GLA_CHUNK

    @pl.when(pl.program_id(1) == 0)
    def _():
        st_ref[...] = jnp.zeros(st_ref.shape, F32)

    q = q_ref[...].astype(F32)
    k = k_ref[...].astype(F32)
    v = v_ref[...]
    z = _dot(sm_ref[...].astype(BF16), wg_ref[...].astype(BF16)) + bg_ref[...]
    g = _log_sigmoid(z) * (1.0 / GLA_TAU)

    t_idx = lax.broadcasted_iota(jnp.int32, (c, c), 0)
    s_idx = lax.broadcasted_iota(jnp.int32, (c, c), 1)
    tril = jnp.where(s_idx <= t_idx, 1.0, 0.0).astype(BF16)
    hi, mid, lo = _split3(g)
    b = _dot(tril, hi) + _dot(tril, mid) + _dot(tril, lo)

    sub8 = t_idx & 7
    att = jnp.where(t_idx == s_idx, _dot_nt(q.astype(BF16), k.astype(BF16)), 0.0)
    n = c
    while n >= 2:
        upper = (t_idx & (n - 1)) >= n // 2
        ref = _segment_reference_rows(b, n, sub8)
        qs = jnp.where(upper, q * jnp.exp(jnp.minimum(b - ref, 0.0)), 0.0).astype(BF16)
        ks = jnp.where(upper, 0.0, k * jnp.exp(jnp.minimum(ref - b, 0.0))).astype(BF16)
        a = _dot_nt(qs, ks)
        if n < c:
            shift = n.bit_length() - 1
            a = jnp.where((t_idx >> shift) == (s_idx >> shift), a, 0.0)
        att = att + a
        n //= 2

    st = st_ref[...]
    o = _dot(att.astype(BF16), v) + _dot_nt((q * jnp.exp(b)).astype(BF16), st.astype(BF16))
    b_last = b[c - 1:c, :]
    ke = (k * jnp.exp(b_last - b)).astype(BF16)
    st_ref[...] = st * jnp.exp(b_last) + lax.dot_general(
        v, ke, (((0,), (0,)), ((), ())), preferred_element_type=F32)

    on = o * lax.rsqrt(jnp.mean(o * o, axis=-1, keepdims=True) + EPS) * ng_ref[...]
    r = r_ref[...].astype(F32)
    o_ref[...] = (r * _sigmoid(r) * on).astype(o_ref.dtype)


def _gla(proj, small, wg_pad, b_g, norm_g):
    lp = proj.shape[0]
    c, dk, dv = GLA_CHUNK, GLA_DK, GLA_DV
    return pl.pallas_call(
        _gla_kernel,
        grid=(GLA_HEADS, lp // c),
        in_specs=[pl.BlockSpec((c, dk), lambda h, i: (i, COL_QC // dk + h)),
                  pl.BlockSpec((c, dk), lambda h, i: (i, COL_KC // dk + h)),
                  pl.BlockSpec((c, dv), lambda h, i: (i, COL_VC // dv + h)),
                  pl.BlockSpec((c, dv), lambda h, i: (i, COL_RC // dv + h)),
                  pl.BlockSpec((c, N_SMALL), lambda h, i: (i, 0)),
                  pl.BlockSpec((None, N_SMALL, dk), lambda h, i: (h, 0, 0)),
                  pl.BlockSpec((None, 1, dk), lambda h, i: (h, 0, 0)),
                  pl.BlockSpec((None, 1, dv), lambda h, i: (h, 0, 0))],
        out_specs=pl.BlockSpec((c, dv), lambda h, i: (i, h)),
        out_shape=jax.ShapeDtypeStruct((lp, GLA_HEADS * dv), BF16),
        scratch_shapes=[pltpu.VMEM((dv, dk), F32)],
        compiler_params=_params(("parallel", "arbitrary")),
        name="gla",
    )(proj, proj, proj, proj, small, wg_pad, b_g, norm_g)


def _causal_conv3(ext, w_ref):
    m1 = pltpu.roll(ext, 1, 0)
    m2 = pltpu.roll(ext, 2, 0)
    return (ext[HALO:] * w_ref[2:3, :] + m1[HALO:] * w_ref[1:2, :] + m2[HALO:] * w_ref[0:1, :])


def _merge_kernel(oa_ref, scb_ref, scc_ref, sch_ref, scc_h_ref, sch_h_ref, oc_ref,
                  ga_ref, gb_ref, gc_ref, gbias_ref, cw_ref, wa_ref, wb_ref, wc_ref, o_ref):
    keep = (pl.program_id(0) > 0).astype(F32)
    p_halo = scc_h_ref[...].astype(F32) * sch_h_ref[...].astype(F32) * keep
    p = scc_ref[...].astype(F32) * sch_ref[...].astype(F32)
    conv = _causal_conv3(jnp.concatenate([p_halo, p], axis=0), cw_ref)
    ub = (scb_ref[...].astype(F32) * conv).astype(BF16)
    ya = _dot(oa_ref[...], wa_ref[...])
    yb = _dot(ub, wb_ref[...])
    yc = _dot(oc_ref[...], wc_ref[...])
    mix = (_sigmoid(ga_ref[...].astype(F32) + gbias_ref[0:1, :]) * ya
           + _sigmoid(gb_ref[...].astype(F32) + gbias_ref[1:2, :]) * yb
           + _sigmoid(gc_ref[...].astype(F32) + gbias_ref[2:3, :]) * yc)
    o_ref[...] = mix.astype(o_ref.dtype)


def _merge(proj, oa, oc, gate_b, conv_w, w_a_o, w_b_o, w_c_o, tm):
    lp = proj.shape[0]
    d = D_MODEL
    ch = CONV_CH
    hb = tm // HALO

    def col(c0, width):
        return lambda i: (i, c0 // width)

    def halo(c0):
        return lambda i: (jnp.maximum(i * hb - 1, 0), c0 // ch)

    def const(shape):
        return pl.BlockSpec(shape, lambda i: (0, 0), pipeline_mode=pl.Buffered(1))

    return pl.pallas_call(
        _merge_kernel,
        grid=(lp // tm,),
        in_specs=[pl.BlockSpec((tm, FOX_WIDTH), lambda i: (i, 0)),
                  pl.BlockSpec((tm, ch), col(COL_SCB, ch)),
                  pl.BlockSpec((tm, ch), col(COL_SCC, ch)),
                  pl.BlockSpec((tm, ch), col(COL_SCH, ch)),
                  pl.BlockSpec((HALO, ch), halo(COL_SCC)),
                  pl.BlockSpec((HALO, ch), halo(COL_SCH)),
                  pl.BlockSpec((tm, GLA_HEADS * GLA_DV), lambda i: (i, 0)),
                  pl.BlockSpec((tm, d), col(COL_GA, d)),
                  pl.BlockSpec((tm, d), col(COL_GB, d)),
                  pl.BlockSpec((tm, d), col(COL_GC, d)),
                  const((3, d)),
                  const((3, ch)),
                  const((FOX_WIDTH, d)),
                  const((ch, d)),
                  const((GLA_HEADS * GLA_DV, d))],
        out_specs=pl.BlockSpec((tm, d), lambda i: (i, 0)),
        out_shape=jax.ShapeDtypeStruct((lp, d), BF16),
        compiler_params=_params(("parallel",)),
        name="merge",
    )(oa, proj, proj, proj, proj, proj, oc, proj, proj, proj,
      gate_b.reshape(3, d), conv_w, w_a_o, w_b_o, w_c_o)


def _resid_matmul_kernel(x_ref, w_ref, h_ref, g_ref, ho_ref, no_ref, acc_ref):
    kk = pl.program_id(1)

    @pl.when(kk == 0)
    def _():
        acc_ref[...] = jnp.zeros(acc_ref.shape, F32)

    acc_ref[...] += _dot(x_ref[...], w_ref[...])

    @pl.when(kk == pl.num_programs(1) - 1)
    def _():
        hn = h_ref[...] + acc_ref[...]
        ho_ref[...] = hn
        y = hn * lax.rsqrt(jnp.mean(hn * hn, axis=-1, keepdims=True) + EPS)
        no_ref[...] = (y * g_ref[...]).astype(no_ref.dtype)


def _resid_matmul(x, w, h, g, norm_dtype, tm, tk, name):
    lp, k = x.shape
    d = w.shape[1]
    return pl.pallas_call(
        _resid_matmul_kernel,
        grid=(lp // tm, k // tk),
        in_specs=[pl.BlockSpec((tm, tk), lambda i, kk: (i, kk)),
                  pl.BlockSpec((tk, d), lambda i, kk: (kk, 0)),
                  pl.BlockSpec((tm, d), lambda i, kk: (i, 0)),
                  pl.BlockSpec((1, d), lambda i, kk: (0, 0))],
        out_specs=[pl.BlockSpec((tm, d), lambda i, kk: (i, 0)),
                   pl.BlockSpec((tm, d), lambda i, kk: (i, 0))],
        out_shape=[jax.ShapeDtypeStruct((lp, d), F32),
                   jax.ShapeDtypeStruct((lp, d), norm_dtype)],
        scratch_shapes=[pltpu.VMEM((tm, d), F32)],
        compiler_params=_params(("parallel", "arbitrary")),
        name=name,
    )(x, w, h, g.reshape(1, d))


def _mlp_up_kernel(x_ref, xh_ref, wg_ref, wu_ref, cg_ref, cu_ref, o_ref, *, n_sub):
    rs = x_ref.shape[0] // n_sub
    keep = (pl.program_id(0) > 0).astype(BF16)
    wg = wg_ref[...].astype(BF16)
    wu = wu_ref[...].astype(BF16)
    for s in range(n_sub):
        if s == 0:
            xe = jnp.concatenate([xh_ref[...] * keep, x_ref[:rs, :]], axis=0)
        else:
            xe = x_ref[s * rs - HALO:(s + 1) * rs, :]
        ug = _causal_conv3(_dot(xe, wg), cg_ref)
        uu = _causal_conv3(_dot(xe, wu), cu_ref)
        o_ref[s * rs:(s + 1) * rs, :] = (ug * _sigmoid(ug) * uu).astype(o_ref.dtype)


def _mlp_up(xn, w_up, conv_w, tm, tn):
    lp, d = xn.shape
    nj = D_FF // tn
    hb = tm // HALO
    n_sub = 4 if tm % (4 * HALO) == 0 else 1
    return pl.pallas_call(
        functools.partial(_mlp_up_kernel, n_sub=n_sub),
        grid=(lp // tm, nj),
        in_specs=[pl.BlockSpec((tm, d), lambda i, j: (i, 0)),
                  pl.BlockSpec((HALO, d), lambda i, j: (jnp.maximum(i * hb - 1, 0), 0)),
                  pl.BlockSpec((d, tn), lambda i, j: (0, j)),
                  pl.BlockSpec((d, tn), lambda i, j: (0, nj + j)),
                  pl.BlockSpec((3, tn), lambda i, j: (0, j)),
                  pl.BlockSpec((3, tn), lambda i, j: (0, nj + j))],
        out_specs=pl.BlockSpec((tm, tn), lambda i, j: (i, j)),
        out_shape=jax.ShapeDtypeStruct((lp, D_FF), BF16),
        compiler_params=_params(("parallel", "arbitrary")),
        name="mlp_up",
    )(xn, xn, w_up, w_up, conv_w, conv_w)


def _reorder_w_in(w):
    f0 = 3 * FOX_WIDTH
    f1 = f0 + FOX_HEADS
    r0 = f1 + 3 * CONV_CH + 2 * GLA_HEADS * GLA_DK + 2 * GLA_HEADS * GLA_DV
    r1 = r0 + GLA_RANK
    main = jnp.concatenate([w[:, r1:], w[:, :f0], w[:, f1:r0]], axis=1).astype(BF16)
    pad = jnp.zeros((w.shape[0], N_SMALL - FOX_HEADS - GLA_RANK), w.dtype)
    small = jnp.concatenate([w[:, f0:f1], w[:, r0:r1], pad], axis=1).astype(BF16)
    return main, small


def _main_colscale():
    s = jnp.ones((1, N_MAIN), F32)
    s = s.at[:, COL_QA:COL_QA + FOX_WIDTH].set(FOX_DH ** -0.5 * LOG2_E)
    s = s.at[:, COL_QC:COL_QC + GLA_HEADS * GLA_DK].set(GLA_DK ** -0.5)
    return s


def kernel(x, meta_tokens, norm1_g, w_in, fox_b_f, gate_b, conv_w, gla_w_g2, gla_b_g, gla_norm_g,
           w_a_o, w_b_o, w_c_o, w_o, norm2_g, w_up, mlp_conv_w, w_down, final_norm_g):
    assert x.shape[0] == 1 and x.shape[2] == D_MODEL
    depth = w_in.shape[0]
    seq = x.shape[1]
    l_real = N_META + seq
    lp = -(-l_real // ROW_TILE) * ROW_TILE
    tm = ROW_TILE
    blk = ROW_TILE

    h = jnp.concatenate([meta_tokens.astype(x.dtype), x[0],
                         jnp.zeros((lp - l_real, D_MODEL), x.dtype)], axis=0)
    colscale = _main_colscale()
    ones_small = jnp.ones((1, N_SMALL), F32)
    xn = _rmsnorm(h, norm1_g[0], BF16, tm)

    out = None
    for l in range(depth):
        w_main, w_small = _reorder_w_in(w_in[l])
        proj = _matmul(xn, w_main, colscale, BF16, _row_tile(lp, 1664), 512, "in_proj")
        small = _matmul(xn, w_small, ones_small, F32, tm, N_SMALL, "in_proj_small")

        f_blocks = small[:, :FOX_HEADS].T.reshape(FOX_HEADS, lp // 128, 128).transpose(1, 0, 2)
        c = _fox_cumsum(f_blocks, fox_b_f[l])
        c_blocks = c.transpose(1, 0, 2).reshape(FOX_HEADS, lp // blk, 1, blk)
        oa = _fox_attention(proj, c_blocks, blk, 320, 64)

        wg = gla_w_g2[l].reshape(GLA_RANK, GLA_HEADS, GLA_DK).transpose(1, 0, 2)
        wg_pad = jnp.zeros((GLA_HEADS, N_SMALL, GLA_DK), F32)
        wg_pad = wg_pad.at[:, SMALL_GLR:SMALL_GLR + GLA_RANK, :].set(wg)
        oc = _gla(proj, small, wg_pad, gla_b_g[l].reshape(GLA_HEADS, 1, GLA_DK),
                  gla_norm_g[l].reshape(GLA_HEADS, 1, GLA_DV))

        mix = _merge(proj, oa, oc, gate_b[l], conv_w[l], w_a_o[l].astype(BF16),
                     w_b_o[l].astype(BF16), w_c_o[l].astype(BF16), 320)
        h, xn2 = _resid_matmul(mix, w_o[l].astype(BF16), h, norm2_g[l], BF16, 320, D_MODEL, "out_proj")

        act = _mlp_up(xn2, w_up[l], mlp_conv_w[l], _row_tile(lp, 1664), 512)
        last = l == depth - 1
        g_next = final_norm_g if last else norm1_g[l + 1]
        h, xn = _resid_matmul(act, w_down[l].astype(BF16), h, g_next,
                              F32 if last else BF16, tm, 512, "mlp_down")
        out = xn
    return out[N_META:l_real][None]
```

```python
import functools

import jax
import jax.numpy as jnp
from jax import lax
from jax.experimental import pallas as pl
from jax.experimental.pallas import tpu as pltpu

F32 = jnp.float32
BF16 = jnp.bfloat16

D_MODEL = 2048
N_META = 16
EPS = 1e-6
NEG = -1e30
LOG2_E = 1.4426950408889634

FOX_HEADS = 8
FOX_DH = D_MODEL // 16
FOX_WIDTH = FOX_HEADS * FOX_DH
CONV_CH = D_MODEL // 2
GLA_HEADS = 4
GLA_DK = D_MODEL // 16
GLA_DV = D_MODEL // 8
GLA_RANK = 16
GLA_TAU = 16.0
GLA_CHUNK = 128
D_FF = D_MODEL * 11 // 4

COL_GA = 0
COL_GB = COL_GA + D_MODEL
COL_GC = COL_GB + D_MODEL
COL_QA = COL_GC + D_MODEL
COL_KA = COL_QA + FOX_WIDTH
COL_VA = COL_KA + FOX_WIDTH
COL_SCB = COL_VA + FOX_WIDTH
COL_SCC = COL_SCB + CONV_CH
COL_SCH = COL_SCC + CONV_CH
COL_QC = COL_SCH + CONV_CH
COL_KC = COL_QC + GLA_HEADS * GLA_DK
COL_VC = COL_KC + GLA_HEADS * GLA_DK
COL_RC = COL_VC + GLA_HEADS * GLA_DV
N_MAIN = COL_RC + GLA_HEADS * GLA_DV
N_SMALL = 128
SMALL_GLR = FOX_HEADS

ROW_TILE = 640
HALO = 16
VMEM_LIMIT = 56 * 1024 * 1024


def _row_tile(lp, cap):
    return max(t for t in range(128, min(cap, lp) + 1, 128) if lp % t == 0)


def _params(sem, vmem=VMEM_LIMIT):
    return pltpu.CompilerParams(dimension_semantics=sem, vmem_limit_bytes=vmem)


def _sigmoid(x):
    return 1.0 / (1.0 + jnp.exp(-x))


def _log_sigmoid(x):
    return jnp.minimum(x, 0.0) - jnp.log(1.0 + jnp.exp(-jnp.abs(x)))


def _dot(a, b):
    return jnp.dot(a, b, preferred_element_type=F32)


def _dot_nt(a, b):
    return lax.dot_general(a, b, (((1,), (1,)), ((), ())), preferred_element_type=F32)


def _split3(x):
    hi = x.astype(BF16)
    r1 = x - hi.astype(F32)
    mid = r1.astype(BF16)
    lo = (r1 - mid.astype(F32)).astype(BF16)
    return hi, mid, lo


def _embed_kernel(x_ref, xp_ref, meta_ref, g_ref, h_ref, n_ref, *, l_real):
    i = pl.program_id(0)
    tm = h_ref.shape[0]
    top = jnp.where(i == 0, meta_ref[...], xp_ref[...])
    hb = jnp.concatenate([top, x_ref[:tm - N_META, :]], axis=0)
    row = i * tm + lax.broadcasted_iota(jnp.int32, (tm, 1), 0)
    hb = jnp.where(row < l_real, hb, 0.0)
    h_ref[...] = hb
    y = hb * lax.rsqrt(jnp.mean(hb * hb, axis=-1, keepdims=True) + EPS)
    n_ref[...] = (y * g_ref[...]).astype(n_ref.dtype)


def _embed(x2, meta, g, lp, tm):
    seq, d = x2.shape
    last_x_block = (seq - 1) // tm
    mb = tm // N_META
    return pl.pallas_call(
        functools.partial(_embed_kernel, l_real=N_META + seq),
        grid=(lp // tm,),
        in_specs=[pl.BlockSpec((tm, d), lambda i: (jnp.minimum(i, last_x_block), 0)),
                  pl.BlockSpec((N_META, d), lambda i: (jnp.maximum(i * mb - 1, 0), 0)),
                  pl.BlockSpec((N_META, d), lambda i: (0, 0)),
                  pl.BlockSpec((1, d), lambda i: (0, 0))],
        out_specs=[pl.BlockSpec((tm, d), lambda i: (i, 0)),
                   pl.BlockSpec((tm, d), lambda i: (i, 0))],
        out_shape=[jax.ShapeDtypeStruct((lp, d), F32),
                   jax.ShapeDtypeStruct((lp, d), BF16)],
        compiler_params=_params(("parallel",)),
        name="embed",
    )(x2, x2, meta, g.reshape(1, d))


def _matmul_kernel(x_ref, w_ref, s_ref, o_ref):
    o_ref[...] = (_dot(x_ref[...], w_ref[...]) * s_ref[...]).astype(o_ref.dtype)


def _matmul(x, w, colscale, out_dtype, tm, tn, name):
    lp, k = x.shape
    n = w.shape[1]
    return pl.pallas_call(
        _matmul_kernel,
        grid=(lp // tm, n // tn),
        in_specs=[pl.BlockSpec((tm, k), lambda i, j: (i, 0)),
                  pl.BlockSpec((k, tn), lambda i, j: (0, j)),
                  pl.BlockSpec((1, tn), lambda i, j: (0, j))],
        out_specs=pl.BlockSpec((tm, tn), lambda i, j: (i, j)),
        out_shape=jax.ShapeDtypeStruct((lp, n), out_dtype),
        compiler_params=_params(("parallel", "arbitrary")),
        name=name,
    )(x, w, colscale)


def _fox_cumsum_kernel(f_ref, b_ref, c_ref):
    nblk = f_ref.shape[0]
    row = lax.broadcasted_iota(jnp.int32, (128, 128), 0)
    col = lax.broadcasted_iota(jnp.int32, (128, 128), 1)
    triu = jnp.where(row <= col, 1.0, 0.0).astype(BF16)

    def body(j, carry):
        logf = _log_sigmoid(f_ref[j] + b_ref[...]) * LOG2_E
        hi, mid, lo = _split3(logf)
        cs = _dot(hi, triu) + _dot(mid, triu) + _dot(lo, triu) + carry
        c_ref[j] = cs
        return cs[:, 127:128]

    lax.fori_loop(0, nblk, body, jnp.zeros((f_ref.shape[1], 1), F32))


def _fox_cumsum(f_blocks, b_f):
    nblk, nh, _ = f_blocks.shape
    return pl.pallas_call(
        _fox_cumsum_kernel,
        out_shape=jax.ShapeDtypeStruct((nblk, nh, 128), F32),
        name="fox_cumsum",
    )(f_blocks, b_f.reshape(nh, 1))


def _lane_tile(x, n):
    return jnp.concatenate([x] * n, axis=1)


def _fox_kernel(q_ref, k_ref, v_ref, c_ref, o_ref, vx_sc, s_sc, p_sc, a_sc, m_sc, acc_sc,
                *, blk, part, sub):
    i = pl.program_id(1)
    dh = q_ref.shape[1]

    @pl.when(i == 0)
    def _():
        vx_sc[:, :dh] = v_ref[...]
        vx_sc[:, dh:] = jnp.ones((vx_sc.shape[0], dh), BF16)

    m_sc[...] = jnp.full(m_sc.shape, NEG, F32)
    acc_sc[...] = jnp.zeros(acc_sc.shape, F32)

    def rows_of(j):
        return pl.ds(pl.multiple_of(j * blk, blk), blk)

    def logits(j, slot):
        kj = k_ref[rows_of(j), :]
        for a in range(0, blk, part):
            s_sc[slot, a:a + part, :] = _dot_nt(q_ref[a:a + part, :], kj)

    def softmax(j, slot, masked):
        cj = c_ref[j]

        def biased(r):
            s = s_sc[slot, r:r + sub, :] - cj
            if masked:
                qpos = r + lax.broadcasted_iota(jnp.int32, (sub, blk), 0)
                kpos = lax.broadcasted_iota(jnp.int32, (sub, blk), 1)
                s = jnp.where(kpos <= qpos, s, NEG)
            return s

        for r in range(0, blk, sub):
            m_prev = m_sc[r:r + sub, :]
            m_new = jnp.maximum(m_prev, jnp.max(biased(r), axis=-1, keepdims=True))
            a_sc[slot, r:r + sub, :] = jnp.exp2(m_prev - m_new)
            m_sc[r:r + sub, :] = m_new
        for r in range(0, blk, sub):
            m_new = _lane_tile(m_sc[r:r + sub, :], blk // 128)
            p_sc[slot, r:r + sub, :] = jnp.exp2(biased(r) - m_new).astype(BF16)

    def values(j, slot):
        vj = vx_sc[rows_of(j), :]
        for a in range(0, blk, part):
            acc_sc[a:a + part, :] = (acc_sc[a:a + part, :] * _lane_tile(a_sc[slot, a:a + part, :], 2)
                                     + _dot(p_sc[slot, a:a + part, :], vj))

    def step(j, slot, masked, prefetch, prev):
        if prefetch:
            logits(j + 1, 1 - slot)
        if prev:
            values(j - 1, 1 - slot)
        softmax(j, slot, masked)

    def step_pair(t, _):
        step(2 * t + 1, 1, masked=False, prefetch=True, prev=True)
        step(2 * t + 2, 0, masked=False, prefetch=True, prev=True)
        return 0

    logits(0, 0)

    @pl.when(i == 0)
    def _():
        softmax(0, 0, masked=True)
        values(0, 0)

    @pl.when(i > 0)
    def _():
        step(0, 0, masked=False, prefetch=True, prev=False)
        lax.fori_loop(0, (i - 1) // 2, step_pair, 0)

        @pl.when(i % 2 == 0)
        def _():
            step(i - 1, 1, masked=False, prefetch=True, prev=True)
            step(i, 0, masked=True, prefetch=False, prev=True)
            values(i, 0)

        @pl.when(i % 2 == 1)
        def _():
            step(i, 1, masked=True, prefetch=False, prev=True)
            values(i, 1)

    o_ref[...] = (acc_sc[:, :dh] / acc_sc[:, dh:]).astype(o_ref.dtype)


def _fox_attention(proj, c_blocks, blk, part, sub):
    lp = proj.shape[0]
    nb = lp // blk
    dh = FOX_DH
    kern = functools.partial(_fox_kernel, blk=blk, part=part, sub=sub)
    return pl.pallas_call(
        kern,
        grid=(FOX_HEADS, nb),
        in_specs=[pl.BlockSpec((blk, dh), lambda h, i: (i, COL_QA // dh + h)),
                  pl.BlockSpec((lp, dh), lambda h, i: (0, COL_KA // dh + h)),
                  pl.BlockSpec((lp, dh), lambda h, i: (0, COL_VA // dh + h)),
                  pl.BlockSpec((None, nb, 1, blk), lambda h, i: (h, 0, 0, 0))],
        out_specs=pl.BlockSpec((blk, dh), lambda h, i: (i, h)),
        out_shape=jax.ShapeDtypeStruct((lp, FOX_WIDTH), BF16),
        scratch_shapes=[pltpu.VMEM((lp, 2 * dh), BF16),
                        pltpu.VMEM((2, blk, blk), F32),
                        pltpu.VMEM((2, blk, blk), BF16),
                        pltpu.VMEM((2, blk, 128), F32),
                        pltpu.VMEM((blk, 128), F32),
                        pltpu.VMEM((blk, 2 * dh), F32)],
        compiler_params=_params(("arbitrary", "arbitrary")),
        name="fox_attention",
    )(proj, proj, proj, c_blocks)


def _segment_reference_rows(b, n, sub8):
    c = b.shape[0]
    half = n // 2
    if n >= 16:
        pieces = [jnp.broadcast_to(b[m * n + half - 1:m * n + half, :], (n, b.shape[1]))
                  for m in range(c // n)]
        return pieces[0] if len(pieces) == 1 else jnp.concatenate(pieces, axis=0)
    b3 = b.reshape(c // 8, 8, b.shape[1])

    def bc(r):
        return jnp.broadcast_to(b3[:, r:r + 1, :], b3.shape).reshape(b.shape)

    out = bc(8 - n + half - 1)
    for start in range(8 - 2 * n, -1, -n):
        out = jnp.where(sub8 < start + n, bc(start + half - 1), out)
    return out


def _gla_kernel(q_ref, k_ref, v_ref, r_ref, sm_ref, wg_ref, bg_ref, ng_ref, o_ref, st_ref):
    c, dk, dv = GLA_CHUNK, GLA_DK, GLA_DV

    @pl.when(pl.program_id(0) == 0)
    def _():
        st_ref[...] = jnp.zeros(st_ref.shape, F32)

    t_idx = lax.broadcasted_iota(jnp.int32, (c, c), 0)
    s_idx = lax.broadcasted_iota(jnp.int32, (c, c), 1)
    tril = jnp.where(s_idx <= t_idx, 1.0, 0.0).astype(BF16)
    sub8 = t_idx & 7
    sm = sm_ref[...].astype(BF16)

    for h in range(GLA_HEADS):
        q = q_ref[:, h * dk:(h + 1) * dk].astype(F32)
        k = k_ref[:, h * dk:(h + 1) * dk].astype(F32)
        v = v_ref[:, h * dv:(h + 1) * dv]
        z = _dot(sm, wg_ref[h].astype(BF16)) + bg_ref[h]
        g = _log_sigmoid(z) * (1.0 / GLA_TAU)
        hi, mid, lo = _split3(g)
        b = _dot(tril, hi) + _dot(tril, mid) + _dot(tril, lo)

        att = jnp.where(t_idx == s_idx, _dot_nt(q.astype(BF16), k.astype(BF16)), 0.0)
        n = c
        while n >= 2:
            upper = (t_idx & (n - 1)) >= n // 2
            ref = _segment_reference_rows(b, n, sub8)
            qs = jnp.where(upper, q * jnp.exp(jnp.minimum(b - ref, 0.0)), 0.0).astype(BF16)
            ks = jnp.where(upper, 0.0, k * jnp.exp(jnp.minimum(ref - b, 0.0))).astype(BF16)
            a = _dot_nt(qs, ks)
            if n < c:
                shift = n.bit_length() - 1
                a = jnp.where((t_idx >> shift) == (s_idx >> shift), a, 0.0)
            att = att + a
            n //= 2

        st = st_ref[h]
        o = _dot(att.astype(BF16), v) + _dot_nt((q * jnp.exp(b)).astype(BF16), st.astype(BF16))
        b_last = b[c - 1:c, :]
        ke = (k * jnp.exp(b_last - b)).astype(BF16)
        st_ref[h] = st * jnp.exp(b_last) + lax.dot_general(
            v, ke, (((0,), (0,)), ((), ())), preferred_element_type=F32)

        on = o * lax.rsqrt(jnp.mean(o * o, axis=-1, keepdims=True) + EPS) * ng_ref[h]
        r = r_ref[:, h * dv:(h + 1) * dv].astype(F32)
        o_ref[:, h * dv:(h + 1) * dv] = (r * _sigmoid(r) * on).astype(o_ref.dtype)


def _gla(proj, small, wg_pad, b_g, norm_g):
    lp = proj.shape[0]
    c, dk, dv, nh = GLA_CHUNK, GLA_DK, GLA_DV, GLA_HEADS
    assert dk == c
    return pl.pallas_call(
        _gla_kernel,
        grid=(lp // c,),
        in_specs=[pl.BlockSpec((c, nh * dk), lambda i: (i, COL_QC // (nh * dk))),
                  pl.BlockSpec((c, nh * dk), lambda i: (i, COL_KC // (nh * dk))),
                  pl.BlockSpec((c, nh * dv), lambda i: (i, COL_VC // (nh * dv))),
                  pl.BlockSpec((c, nh * dv), lambda i: (i, COL_RC // (nh * dv))),
                  pl.BlockSpec((c, N_SMALL), lambda i: (i, 0)),
                  pl.BlockSpec((nh, N_SMALL, dk), lambda i: (0, 0, 0)),
                  pl.BlockSpec((nh, 1, dk), lambda i: (0, 0, 0)),
                  pl.BlockSpec((nh, 1, dv), lambda i: (0, 0, 0))],
        out_specs=pl.BlockSpec((c, nh * dv), lambda i: (i, 0)),
        out_shape=jax.ShapeDtypeStruct((lp, nh * dv), BF16),
        scratch_shapes=[pltpu.VMEM((nh, dv, dk), F32)],
        compiler_params=_params(("arbitrary",)),
        name="gla",
    )(proj, proj, proj, proj, small, wg_pad, b_g, norm_g)


def _causal_conv3(ext, w_ref):
    m1 = pltpu.roll(ext, 1, 0)
    m2 = pltpu.roll(ext, 2, 0)
    return (ext[HALO:] * w_ref[2:3, :] + m1[HALO:] * w_ref[1:2, :] + m2[HALO:] * w_ref[0:1, :])


def _merge_kernel(oa_ref, scb_ref, scc_ref, sch_ref, scc_h_ref, sch_h_ref, oc_ref,
                  ga_ref, gb_ref, gc_ref, gbias_ref, cw_ref, wa_ref, wb_ref, wc_ref, o_ref):
    keep = (pl.program_id(0) > 0).astype(F32)
    p_halo = scc_h_ref[...].astype(F32) * sch_h_ref[...].astype(F32) * keep
    p = scc_ref[...].astype(F32) * sch_ref[...].astype(F32)
    conv = _causal_conv3(jnp.concatenate([p_halo, p], axis=0), cw_ref)
    ub = (scb_ref[...].astype(F32) * conv).astype(BF16)
    ya = _dot(oa_ref[...], wa_ref[...])
    yb = _dot(ub, wb_ref[...])
    yc = _dot(oc_ref[...], wc_ref[...])
    mix = (_sigmoid(ga_ref[...].astype(F32) + gbias_ref[0:1, :]) * ya
           + _sigmoid(gb_ref[...].astype(F32) + gbias_ref[1:2, :]) * yb
           + _sigmoid(gc_ref[...].astype(F32) + gbias_ref[2:3, :]) * yc)
    o_ref[...] = mix.astype(o_ref.dtype)


def _merge(proj, oa, oc, gate_b, conv_w, w_a_o, w_b_o, w_c_o, tm):
    lp = proj.shape[0]
    d = D_MODEL
    ch = CONV_CH
    hb = tm // HALO

    def col(c0, width):
        return lambda i: (i, c0 // width)

    def halo(c0):
        return lambda i: (jnp.maximum(i * hb - 1, 0), c0 // ch)

    def const(shape):
        return pl.BlockSpec(shape, lambda i: (0, 0), pipeline_mode=pl.Buffered(1))

    return pl.pallas_call(
        _merge_kernel,
        grid=(lp // tm,),
        in_specs=[pl.BlockSpec((tm, FOX_WIDTH), lambda i: (i, 0)),
                  pl.BlockSpec((tm, ch), col(COL_SCB, ch)),
                  pl.BlockSpec((tm, ch), col(COL_SCC, ch)),
                  pl.BlockSpec((tm, ch), col(COL_SCH, ch)),
                  pl.BlockSpec((HALO, ch), halo(COL_SCC)),
                  pl.BlockSpec((HALO, ch), halo(COL_SCH)),
                  pl.BlockSpec((tm, GLA_HEADS * GLA_DV), lambda i: (i, 0)),
                  pl.BlockSpec((tm, d), col(COL_GA, d)),
                  pl.BlockSpec((tm, d), col(COL_GB, d)),
                  pl.BlockSpec((tm, d), col(COL_GC, d)),
                  const((3, d)),
                  const((3, ch)),
                  const((FOX_WIDTH, d)),
                  const((ch, d)),
                  const((GLA_HEADS * GLA_DV, d))],
        out_specs=pl.BlockSpec((tm, d), lambda i: (i, 0)),
        out_shape=jax.ShapeDtypeStruct((lp, d), BF16),
        compiler_params=_params(("parallel",)),
        name="merge",
    )(oa, proj, proj, proj, proj, proj, oc, proj, proj, proj,
      gate_b.reshape(3, d), conv_w, w_a_o, w_b_o, w_c_o)


def _resid_matmul_kernel(x_ref, w_ref, h_ref, g_ref, ho_ref, no_ref, acc_ref):
    kk = pl.program_id(1)

    @pl.when(kk == 0)
    def _():
        acc_ref[...] = jnp.zeros(acc_ref.shape, F32)

    acc_ref[...] += _dot(x_ref[...], w_ref[...])

    @pl.when(kk == pl.num_programs(1) - 1)
    def _():
        hn = h_ref[...] + acc_ref[...]
        ho_ref[...] = hn
        y = hn * lax.rsqrt(jnp.mean(hn * hn, axis=-1, keepdims=True) + EPS)
        no_ref[...] = (y * g_ref[...]).astype(no_ref.dtype)


def _resid_matmul(x, w, h, g, norm_dtype, tm, tk, name):
    lp, k = x.shape
    d = w.shape[1]
    return pl.pallas_call(
        _resid_matmul_kernel,
        grid=(lp // tm, k // tk),
        in_specs=[pl.BlockSpec((tm, tk), lambda i, kk: (i, kk)),
                  pl.BlockSpec((tk, d), lambda i, kk: (kk, 0)),
                  pl.BlockSpec((tm, d), lambda i, kk: (i, 0)),
                  pl.BlockSpec((1, d), lambda i, kk: (0, 0))],
        out_specs=[pl.BlockSpec((tm, d), lambda i, kk: (i, 0)),
                   pl.BlockSpec((tm, d), lambda i, kk: (i, 0))],
        out_shape=[jax.ShapeDtypeStruct((lp, d), F32),
                   jax.ShapeDtypeStruct((lp, d), norm_dtype)],
        scratch_shapes=[pltpu.VMEM((tm, d), F32)],
        compiler_params=_params(("parallel", "arbitrary")),
        name=name,
    )(x, w, h, g.reshape(1, d))


def _mlp_up_kernel(x_ref, xh_ref, wg_ref, wu_ref, cg_ref, cu_ref, o_ref, *, n_sub):
    rs = x_ref.shape[0] // n_sub
    keep = (pl.program_id(0) > 0).astype(BF16)
    wg = wg_ref[...].astype(BF16)
    wu = wu_ref[...].astype(BF16)
    for s in range(n_sub):
        if s == 0:
            xe = jnp.concatenate([xh_ref[...] * keep, x_ref[:rs, :]], axis=0)
        else:
            xe = x_ref[s * rs - HALO:(s + 1) * rs, :]
        ug = _causal_conv3(_dot(xe, wg), cg_ref)
        uu = _causal_conv3(_dot(xe, wu), cu_ref)
        o_ref[s * rs:(s + 1) * rs, :] = (ug * _sigmoid(ug) * uu).astype(o_ref.dtype)


def _mlp_up(xn, w_up, layer, conv_w, tm, tn):
    lp, d = xn.shape
    nj = D_FF // tn
    hb = tm // HALO
    n_sub = 4 if tm % (4 * HALO) == 0 else 1
    return pl.pallas_call(
        functools.partial(_mlp_up_kernel, n_sub=n_sub),
        grid=(lp // tm, nj),
        in_specs=[pl.BlockSpec((tm, d), lambda i, j: (i, 0)),
                  pl.BlockSpec((HALO, d), lambda i, j: (jnp.maximum(i * hb - 1, 0), 0)),
                  pl.BlockSpec((None, d, tn), lambda i, j: (layer, 0, j)),
                  pl.BlockSpec((None, d, tn), lambda i, j: (layer, 0, nj + j)),
                  pl.BlockSpec((3, tn), lambda i, j: (0, j)),
                  pl.BlockSpec((3, tn), lambda i, j: (0, nj + j))],
        out_specs=pl.BlockSpec((tm, tn), lambda i, j: (i, j)),
        out_shape=jax.ShapeDtypeStruct((lp, D_FF), BF16),
        compiler_params=_params(("parallel", "arbitrary")),
        name="mlp_up",
    )(xn, xn, w_up, w_up, conv_w, conv_w)


def _reorder_w_in(w):
    f0 = 3 * FOX_WIDTH
    f1 = f0 + FOX_HEADS
    r0 = f1 + 3 * CONV_CH + 2 * GLA_HEADS * GLA_DK + 2 * GLA_HEADS * GLA_DV
    r1 = r0 + GLA_RANK
    main = jnp.concatenate([w[:, r1:], w[:, :f0], w[:, f1:r0]], axis=1).astype(BF16)
    pad = jnp.zeros((w.shape[0], N_SMALL - FOX_HEADS - GLA_RANK), w.dtype)
    small = jnp.concatenate([w[:, f0:f1], w[:, r0:r1], pad], axis=1).astype(BF16)
    return main, small


def _main_colscale():
    s = jnp.ones((1, N_MAIN), F32)
    s = s.at[:, COL_QA:COL_QA + FOX_WIDTH].set(FOX_DH ** -0.5 * LOG2_E)
    s = s.at[:, COL_QC:COL_QC + GLA_HEADS * GLA_DK].set(GLA_DK ** -0.5)
    return s


def kernel(x, meta_tokens, norm1_g, w_in, fox_b_f, gate_b, conv_w, gla_w_g2, gla_b_g, gla_norm_g,
           w_a_o, w_b_o, w_c_o, w_o, norm2_g, w_up, mlp_conv_w, w_down, final_norm_g):
    assert x.shape[0] == 1 and x.shape[2] == D_MODEL
    depth = w_in.shape[0]
    seq = x.shape[1]
    l_real = N_META + seq
    lp = -(-l_real // ROW_TILE) * ROW_TILE
    tm = ROW_TILE
    blk = ROW_TILE

    h, xn = _embed(x[0], meta_tokens.astype(x.dtype), norm1_g[0], lp, tm)
    colscale = _main_colscale()
    ones_small = jnp.ones((1, N_SMALL), F32)

    out = None
    for l in range(depth):
        w_main, w_small = _reorder_w_in(w_in[l])
        proj = _matmul(xn, w_main, colscale, BF16, _row_tile(lp, 1664), 512, "in_proj")
        small = _matmul(xn, w_small, ones_small, F32, tm, N_SMALL, "in_proj_small")

        f_blocks = small[:, :FOX_HEADS].T.reshape(FOX_HEADS, lp // 128, 128).transpose(1, 0, 2)
        c = _fox_cumsum(f_blocks, fox_b_f[l])
        c_blocks = c.transpose(1, 0, 2).reshape(FOX_HEADS, lp // blk, 1, blk)
        oa = _fox_attention(proj, c_blocks, blk, 320, 64)

        wg = gla_w_g2[l].reshape(GLA_RANK, GLA_HEADS, GLA_DK).transpose(1, 0, 2)
        wg_pad = jnp.zeros((GLA_HEADS, N_SMALL, GLA_DK), F32)
        wg_pad = wg_pad.at[:, SMALL_GLR:SMALL_GLR + GLA_RANK, :].set(wg)
        oc = _gla(proj, small, wg_pad, gla_b_g[l].reshape(GLA_HEADS, 1, GLA_DK),
                  gla_norm_g[l].reshape(GLA_HEADS, 1, GLA_DV))

        mix = _merge(proj, oa, oc, gate_b[l], conv_w[l], w_a_o[l].astype(BF16),
                     w_b_o[l].astype(BF16), w_c_o[l].astype(BF16), 320)
        h, xn2 = _resid_matmul(mix, w_o[l].astype(BF16), h, norm2_g[l], BF16, 320, D_MODEL, "out_proj")

        act = _mlp_up(xn2, w_up, l, mlp_conv_w[l], _row_tile(lp, 1664), 512)
        last = l == depth - 1
        g_next = final_norm_g if last else norm1_g[l + 1]
        h, xn = _resid_matmul(act, w_down[l].astype(BF16), h, g_next,
                              F32 if last else BF16, tm, 512, "mlp_down")
        out = xn
    return out[N_META:l_real][None]
```

```python
import functools

import jax
import jax.numpy as jnp
from jax import lax
from jax.experimental import pallas as pl
from jax.experimental.pallas import tpu as pltpu

F32 = jnp.float32
BF16 = jnp.bfloat16

D_MODEL = 2048
N_META = 16
EPS = 1e-6
NEG = -1e30
LOG2_E = 1.4426950408889634

FOX_HEADS = 8
FOX_DH = D_MODEL // 16
FOX_WIDTH = FOX_HEADS * FOX_DH
CONV_CH = D_MODEL // 2
GLA_HEADS = 4
GLA_DK = D_MODEL // 16
GLA_DV = D_MODEL // 8
GLA_RANK = 16
GLA_TAU = 16.0
GLA_CHUNK = 128
D_FF = D_MODEL * 11 // 4

COL_GA = 0
COL_GB = COL_GA + D_MODEL
COL_GC = COL_GB + D_MODEL
COL_QA = COL_GC + D_MODEL
COL_KA = COL_QA + FOX_WIDTH
COL_VA = COL_KA + FOX_WIDTH
COL_SCB = COL_VA + FOX_WIDTH
COL_SCC = COL_SCB + CONV_CH
COL_SCH = COL_SCC + CONV_CH
COL_QC = COL_SCH + CONV_CH
COL_KC = COL_QC + GLA_HEADS * GLA_DK
COL_VC = COL_KC + GLA_HEADS * GLA_DK
COL_RC = COL_VC + GLA_HEADS * GLA_DV
N_MAIN = COL_RC + GLA_HEADS * GLA_DV
N_SMALL = 128
SMALL_GLR = FOX_HEADS

ROW_TILE = 640
HALO = 16
VMEM_LIMIT = 56 * 1024 * 1024


def _row_tile(lp, cap, granule=128):
    return max(t for t in range(granule, min(cap, lp) + 1, granule) if lp % t == 0)


def _params(sem, vmem=VMEM_LIMIT):
    return pltpu.CompilerParams(dimension_semantics=sem, vmem_limit_bytes=vmem)


def _sigmoid(x):
    return 1.0 / (1.0 + jnp.exp(-x))


def _log_sigmoid(x):
    return jnp.minimum(x, 0.0) - jnp.log(1.0 + jnp.exp(-jnp.abs(x)))


def _dot(a, b):
    return jnp.dot(a, b, preferred_element_type=F32)


def _dot_nt(a, b):
    return lax.dot_general(a, b, (((1,), (1,)), ((), ())), preferred_element_type=F32)


def _split3(x):
    hi = x.astype(BF16)
    r1 = x - hi.astype(F32)
    mid = r1.astype(BF16)
    lo = (r1 - mid.astype(F32)).astype(BF16)
    return hi, mid, lo


def _embed_kernel(x_ref, xp_ref, meta_ref, g_ref, h_ref, n_ref, *, l_real):
    i = pl.program_id(0)
    tm = h_ref.shape[0]
    top = jnp.where(i == 0, meta_ref[...], xp_ref[...])
    hb = jnp.concatenate([top, x_ref[:tm - N_META, :]], axis=0)
    row = i * tm + lax.broadcasted_iota(jnp.int32, (tm, 1), 0)
    hb = jnp.where(row < l_real, hb, 0.0)
    h_ref[...] = hb
    y = hb * lax.rsqrt(jnp.mean(hb * hb, axis=-1, keepdims=True) + EPS)
    n_ref[...] = (y * g_ref[...]).astype(n_ref.dtype)


def _embed(x2, meta, g, lp, tm):
    seq, d = x2.shape
    last_x_block = (seq - 1) // tm
    mb = tm // N_META
    return pl.pallas_call(
        functools.partial(_embed_kernel, l_real=N_META + seq),
        grid=(lp // tm,),
        in_specs=[pl.BlockSpec((tm, d), lambda i: (jnp.minimum(i, last_x_block), 0)),
                  pl.BlockSpec((N_META, d), lambda i: (jnp.maximum(i * mb - 1, 0), 0)),
                  pl.BlockSpec((N_META, d), lambda i: (0, 0)),
                  pl.BlockSpec((1, d), lambda i: (0, 0))],
        out_specs=[pl.BlockSpec((tm, d), lambda i: (i, 0)),
                   pl.BlockSpec((tm, d), lambda i: (i, 0))],
        out_shape=[jax.ShapeDtypeStruct((lp, d), F32),
                   jax.ShapeDtypeStruct((lp, d), BF16)],
        compiler_params=_params(("parallel",)),
        name="embed",
    )(x2, x2, meta, g.reshape(1, d))


def _matmul_kernel(x_ref, w_ref, s_ref, o_ref):
    o_ref[...] = (_dot(x_ref[...], w_ref[...]) * s_ref[...]).astype(o_ref.dtype)


def _matmul(x, w, colscale, out_dtype, tm, tn, name):
    lp, k = x.shape
    n = w.shape[1]
    return pl.pallas_call(
        _matmul_kernel,
        grid=(lp // tm, n // tn),
        in_specs=[pl.BlockSpec((tm, k), lambda i, j: (i, 0)),
                  pl.BlockSpec((k, tn), lambda i, j: (0, j)),
                  pl.BlockSpec((1, tn), lambda i, j: (0, j))],
        out_specs=pl.BlockSpec((tm, tn), lambda i, j: (i, j)),
        out_shape=jax.ShapeDtypeStruct((lp, n), out_dtype),
        compiler_params=_params(("parallel", "arbitrary")),
        name=name,
    )(x, w, colscale)


def _fox_cumsum_kernel(f_ref, b_ref, c_ref):
    nblk = f_ref.shape[0]
    row = lax.broadcasted_iota(jnp.int32, (128, 128), 0)
    col = lax.broadcasted_iota(jnp.int32, (128, 128), 1)
    triu = jnp.where(row <= col, 1.0, 0.0).astype(BF16)

    def body(j, carry):
        logf = _log_sigmoid(f_ref[j] + b_ref[...]) * LOG2_E
        hi, mid, lo = _split3(logf)
        cs = _dot(hi, triu) + _dot(mid, triu) + _dot(lo, triu) + carry
        c_ref[j] = cs
        return cs[:, 127:128]

    lax.fori_loop(0, nblk, body, jnp.zeros((f_ref.shape[1], 1), F32))


def _fox_cumsum(f_blocks, b_f):
    nblk, nh, _ = f_blocks.shape
    return pl.pallas_call(
        _fox_cumsum_kernel,
        out_shape=jax.ShapeDtypeStruct((nblk, nh, 128), F32),
        name="fox_cumsum",
    )(f_blocks, b_f.reshape(nh, 1))


def _lane_tile(x, n):
    return jnp.concatenate([x] * n, axis=1)


def _fox_kernel(q_ref, k_ref, v_ref, c_ref, o_ref, vx_sc, s_sc, p_sc, a_sc, m_sc, acc_sc,
                *, blk, part, sub):
    i = pl.program_id(1)
    dh = q_ref.shape[1]

    @pl.when(i == 0)
    def _():
        vx_sc[:, :dh] = v_ref[...]
        vx_sc[:, dh:] = jnp.ones((vx_sc.shape[0], dh), BF16)

    m_sc[...] = jnp.full(m_sc.shape, NEG, F32)
    acc_sc[...] = jnp.zeros(acc_sc.shape, F32)

    def rows_of(j):
        return pl.ds(pl.multiple_of(j * blk, blk), blk)

    def logits(j, slot):
        kj = k_ref[rows_of(j), :]
        for a in range(0, blk, part):
            s_sc[slot, a:a + part, :] = _dot_nt(q_ref[a:a + part, :], kj)

    def softmax(j, slot, masked):
        cj = c_ref[j]

        def biased(r):
            s = s_sc[slot, r:r + sub, :] - cj
            if masked:
                qpos = r + lax.broadcasted_iota(jnp.int32, (sub, blk), 0)
                kpos = lax.broadcasted_iota(jnp.int32, (sub, blk), 1)
                s = jnp.where(kpos <= qpos, s, NEG)
            return s

        for r in range(0, blk, sub):
            m_prev = m_sc[r:r + sub, :]
            m_new = jnp.maximum(m_prev, jnp.max(biased(r), axis=-1, keepdims=True))
            a_sc[slot, r:r + sub, :] = jnp.exp2(m_prev - m_new)
            m_sc[r:r + sub, :] = m_new
        for r in range(0, blk, sub):
            m_new = _lane_tile(m_sc[r:r + sub, :], blk // 128)
            p_sc[slot, r:r + sub, :] = jnp.exp2(biased(r) - m_new).astype(BF16)

    def values(j, slot):
        vj = vx_sc[rows_of(j), :]
        for a in range(0, blk, part):
            acc_sc[a:a + part, :] = (acc_sc[a:a + part, :] * _lane_tile(a_sc[slot, a:a + part, :], 2)
                                     + _dot(p_sc[slot, a:a + part, :], vj))

    def step(j, slot, masked, prefetch, prev):
        if prefetch:
            logits(j + 1, 1 - slot)
        if prev:
            values(j - 1, 1 - slot)
        softmax(j, slot, masked)

    def step_pair(t, _):
        step(2 * t + 1, 1, masked=False, prefetch=True, prev=True)
        step(2 * t + 2, 0, masked=False, prefetch=True, prev=True)
        return 0

    logits(0, 0)

    @pl.when(i == 0)
    def _():
        softmax(0, 0, masked=True)
        values(0, 0)

    @pl.when(i > 0)
    def _():
        step(0, 0, masked=False, prefetch=True, prev=False)
        lax.fori_loop(0, (i - 1) // 2, step_pair, 0)

        @pl.when(i % 2 == 0)
        def _():
            step(i - 1, 1, masked=False, prefetch=True, prev=True)
            step(i, 0, masked=True, prefetch=False, prev=True)
            values(i, 0)

        @pl.when(i % 2 == 1)
        def _():
            step(i, 1, masked=True, prefetch=False, prev=True)
            values(i, 1)

    o_ref[...] = (acc_sc[:, :dh] / acc_sc[:, dh:]).astype(o_ref.dtype)


def _fox_attention(proj, c_blocks, blk, part, sub):
    lp = proj.shape[0]
    nb = lp // blk
    dh = FOX_DH
    kern = functools.partial(_fox_kernel, blk=blk, part=part, sub=sub)
    return pl.pallas_call(
        kern,
        grid=(FOX_HEADS, nb),
        in_specs=[pl.BlockSpec((blk, dh), lambda h, i: (i, COL_QA // dh + h)),
                  pl.BlockSpec((lp, dh), lambda h, i: (0, COL_KA // dh + h)),
                  pl.BlockSpec((lp, dh), lambda h, i: (0, COL_VA // dh + h)),
                  pl.BlockSpec((None, nb, 1, blk), lambda h, i: (h, 0, 0, 0))],
        out_specs=pl.BlockSpec((blk, dh), lambda h, i: (i, h)),
        out_shape=jax.ShapeDtypeStruct((lp, FOX_WIDTH), BF16),
        scratch_shapes=[pltpu.VMEM((lp, 2 * dh), BF16),
                        pltpu.VMEM((2, blk, blk), F32),
                        pltpu.VMEM((2, blk, blk), BF16),
                        pltpu.VMEM((2, blk, 128), F32),
                        pltpu.VMEM((blk, 128), F32),
                        pltpu.VMEM((blk, 2 * dh), F32)],
        compiler_params=_params(("arbitrary", "arbitrary")),
        name="fox_attention",
    )(proj, proj, proj, c_blocks)


def _segment_reference_rows(b, n, sub8):
    c = b.shape[0]
    half = n // 2
    if n >= 16:
        pieces = [jnp.broadcast_to(b[m * n + half - 1:m * n + half, :], (n, b.shape[1]))
                  for m in range(c // n)]
        return pieces[0] if len(pieces) == 1 else jnp.concatenate(pieces, axis=0)
    b3 = b.reshape(c // 8, 8, b.shape[1])

    def bc(r):
        return jnp.broadcast_to(b3[:, r:r + 1, :], b3.shape).reshape(b.shape)

    out = bc(8 - n + half - 1)
    for start in range(8 - 2 * n, -1, -n):
        out = jnp.where(sub8 < start + n, bc(start + half - 1), out)
    return out


def _gla_kernel(q_ref, k_ref, v_ref, r_ref, sm_ref, wg_ref, bg_ref, ng_ref, o_ref, st_ref):
    c, dk, dv = GLA_CHUNK, GLA_DK, GLA_DV

    @pl.when(pl.program_id(0) == 0)
    def _():
        st_ref[...] = jnp.zeros(st_ref.shape, F32)

    t_idx = lax.broadcasted_iota(jnp.int32, (c, c), 0)
    s_idx = lax.broadcasted_iota(jnp.int32, (c, c), 1)
    tril = jnp.where(s_idx <= t_idx, 1.0, 0.0).astype(BF16)
    sub8 = t_idx & 7
    sm = sm_ref[...].astype(BF16)

    for h in range(GLA_HEADS):
        q = q_ref[:, h * dk:(h + 1) * dk].astype(F32)
        k = k_ref[:, h * dk:(h + 1) * dk].astype(F32)
        v = v_ref[:, h * dv:(h + 1) * dv]
        z = _dot(sm, wg_ref[h].astype(BF16)) + bg_ref[h]
        g = _log_sigmoid(z) * (1.0 / GLA_TAU)
        hi, mid, lo = _split3(g)
        b = _dot(tril, hi) + _dot(tril, mid) + _dot(tril, lo)

        att = jnp.where(t_idx == s_idx, _dot_nt(q.astype(BF16), k.astype(BF16)), 0.0)
        n = c
        while n >= 2:
            upper = (t_idx & (n - 1)) >= n // 2
            ref = _segment_reference_rows(b, n, sub8)
            qs = jnp.where(upper, q * jnp.exp(jnp.minimum(b - ref, 0.0)), 0.0).astype(BF16)
            ks = jnp.where(upper, 0.0, k * jnp.exp(jnp.minimum(ref - b, 0.0))).astype(BF16)
            a = _dot_nt(qs, ks)
            if n < c:
                shift = n.bit_length() - 1
                a = jnp.where((t_idx >> shift) == (s_idx >> shift), a, 0.0)
            att = att + a
            n //= 2

        st = st_ref[h]
        o = _dot(att.astype(BF16), v) + _dot_nt((q * jnp.exp(b)).astype(BF16), st.astype(BF16))
        b_last = b[c - 1:c, :]
        ke = (k * jnp.exp(b_last - b)).astype(BF16)
        st_ref[h] = st * jnp.exp(b_last) + lax.dot_general(
            v, ke, (((0,), (0,)), ((), ())), preferred_element_type=F32)

        on = o * lax.rsqrt(jnp.mean(o * o, axis=-1, keepdims=True) + EPS) * ng_ref[h]
        r = r_ref[:, h * dv:(h + 1) * dv].astype(F32)
        o_ref[:, h * dv:(h + 1) * dv] = (r * _sigmoid(r) * on).astype(o_ref.dtype)


def _gla(proj, small, wg_pad, b_g, norm_g):
    lp = proj.shape[0]
    c, dk, dv, nh = GLA_CHUNK, GLA_DK, GLA_DV, GLA_HEADS
    assert dk == c
    return pl.pallas_call(
        _gla_kernel,
        grid=(lp // c,),
        in_specs=[pl.BlockSpec((c, nh * dk), lambda i: (i, COL_QC // (nh * dk))),
                  pl.BlockSpec((c, nh * dk), lambda i: (i, COL_KC // (nh * dk))),
                  pl.BlockSpec((c, nh * dv), lambda i: (i, COL_VC // (nh * dv))),
                  pl.BlockSpec((c, nh * dv), lambda i: (i, COL_RC // (nh * dv))),
                  pl.BlockSpec((c, N_SMALL), lambda i: (i, 0)),
                  pl.BlockSpec((nh, N_SMALL, dk), lambda i: (0, 0, 0)),
                  pl.BlockSpec((nh, 1, dk), lambda i: (0, 0, 0)),
                  pl.BlockSpec((nh, 1, dv), lambda i: (0, 0, 0))],
        out_specs=pl.BlockSpec((c, nh * dv), lambda i: (i, 0)),
        out_shape=jax.ShapeDtypeStruct((lp, nh * dv), BF16),
        scratch_shapes=[pltpu.VMEM((nh, dv, dk), F32)],
        compiler_params=_params(("arbitrary",)),
        name="gla",
    )(proj, proj, proj, proj, small, wg_pad, b_g, norm_g)


def _causal_conv3(ext, w_ref):
    m1 = pltpu.roll(ext, 1, 0)
    m2 = pltpu.roll(ext, 2, 0)
    return (ext[HALO:] * w_ref[2:3, :] + m1[HALO:] * w_ref[1:2, :] + m2[HALO:] * w_ref[0:1, :])


def _merge_kernel(oa_ref, scb_ref, scc_ref, sch_ref, scc_h_ref, sch_h_ref, oc_ref,
                  ga_ref, gb_ref, gc_ref, gbias_ref, cw_ref, wa_ref, wb_ref, wc_ref, o_ref):
    keep = (pl.program_id(0) > 0).astype(F32)
    p_halo = scc_h_ref[...].astype(F32) * sch_h_ref[...].astype(F32) * keep
    p = scc_ref[...].astype(F32) * sch_ref[...].astype(F32)
    conv = _causal_conv3(jnp.concatenate([p_halo, p], axis=0), cw_ref)
    ub = (scb_ref[...].astype(F32) * conv).astype(BF16)
    ya = _dot(oa_ref[...], wa_ref[...])
    yb = _dot(ub, wb_ref[...])
    yc = _dot(oc_ref[...], wc_ref[...])
    mix = (_sigmoid(ga_ref[...].astype(F32) + gbias_ref[0:1, :]) * ya
           + _sigmoid(gb_ref[...].astype(F32) + gbias_ref[1:2, :]) * yb
           + _sigmoid(gc_ref[...].astype(F32) + gbias_ref[2:3, :]) * yc)
    o_ref[...] = mix.astype(o_ref.dtype)


def _merge(proj, oa, oc, gate_b, conv_w, w_a_o, w_b_o, w_c_o, tm):
    lp = proj.shape[0]
    d = D_MODEL
    ch = CONV_CH
    hb = tm // HALO

    def col(c0, width):
        return lambda i: (i, c0 // width)

    def halo(c0):
        return lambda i: (jnp.maximum(i * hb - 1, 0), c0 // ch)

    def const(shape):
        return pl.BlockSpec(shape, lambda i: (0, 0), pipeline_mode=pl.Buffered(1))

    return pl.pallas_call(
        _merge_kernel,
        grid=(lp // tm,),
        in_specs=[pl.BlockSpec((tm, FOX_WIDTH), lambda i: (i, 0)),
                  pl.BlockSpec((tm, ch), col(COL_SCB, ch)),
                  pl.BlockSpec((tm, ch), col(COL_SCC, ch)),
                  pl.BlockSpec((tm, ch), col(COL_SCH, ch)),
                  pl.BlockSpec((HALO, ch), halo(COL_SCC)),
                  pl.BlockSpec((HALO, ch), halo(COL_SCH)),
                  pl.BlockSpec((tm, GLA_HEADS * GLA_DV), lambda i: (i, 0)),
                  pl.BlockSpec((tm, d), col(COL_GA, d)),
                  pl.BlockSpec((tm, d), col(COL_GB, d)),
                  pl.BlockSpec((tm, d), col(COL_GC, d)),
                  const((3, d)),
                  const((3, ch)),
                  const((FOX_WIDTH, d)),
                  const((ch, d)),
                  const((GLA_HEADS * GLA_DV, d))],
        out_specs=pl.BlockSpec((tm, d), lambda i: (i, 0)),
        out_shape=jax.ShapeDtypeStruct((lp, d), BF16),
        compiler_params=_params(("parallel",)),
        name="merge",
    )(oa, proj, proj, proj, proj, proj, oc, proj, proj, proj,
      gate_b.reshape(3, d), conv_w, w_a_o, w_b_o, w_c_o)


def _resid_matmul_kernel(x_ref, w_ref, h_ref, g_ref, ho_ref, no_ref, acc_ref):
    kk = pl.program_id(1)

    @pl.when(kk == 0)
    def _():
        acc_ref[...] = jnp.zeros(acc_ref.shape, F32)

    acc_ref[...] += _dot(x_ref[...], w_ref[...])

    @pl.when(kk == pl.num_programs(1) - 1)
    def _():
        hn = h_ref[...] + acc_ref[...]
        ho_ref[...] = hn
        y = hn * lax.rsqrt(jnp.mean(hn * hn, axis=-1, keepdims=True) + EPS)
        no_ref[...] = (y * g_ref[...]).astype(no_ref.dtype)


def _resid_matmul(x, w, h, g, norm_dtype, tm, tk, name):
    lp, k = x.shape
    d = w.shape[1]
    return pl.pallas_call(
        _resid_matmul_kernel,
        grid=(lp // tm, k // tk),
        in_specs=[pl.BlockSpec((tm, tk), lambda i, kk: (i, kk)),
                  pl.BlockSpec((tk, d), lambda i, kk: (kk, 0)),
                  pl.BlockSpec((tm, d), lambda i, kk: (i, 0)),
                  pl.BlockSpec((1, d), lambda i, kk: (0, 0))],
        out_specs=[pl.BlockSpec((tm, d), lambda i, kk: (i, 0)),
                   pl.BlockSpec((tm, d), lambda i, kk: (i, 0))],
        out_shape=[jax.ShapeDtypeStruct((lp, d), F32),
                   jax.ShapeDtypeStruct((lp, d), norm_dtype)],
        scratch_shapes=[pltpu.VMEM((tm, d), F32)],
        compiler_params=_params(("parallel", "arbitrary")),
        name=name,
    )(x, w, h, g.reshape(1, d))


def _matmul_resid_kernel(x_ref, w_ref, h_ref, o_ref):
    o_ref[...] = h_ref[...] + _dot(x_ref[...], w_ref[...])


def _matmul_resid(x, w, h, tm, tn, name):
    lp, k = x.shape
    n = w.shape[1]
    return pl.pallas_call(
        _matmul_resid_kernel,
        grid=(lp // tm, n // tn),
        in_specs=[pl.BlockSpec((tm, k), lambda i, j: (i, 0)),
                  pl.BlockSpec((k, tn), lambda i, j: (0, j)),
                  pl.BlockSpec((tm, tn), lambda i, j: (i, j))],
        out_specs=pl.BlockSpec((tm, tn), lambda i, j: (i, j)),
        out_shape=jax.ShapeDtypeStruct((lp, n), F32),
        compiler_params=_params(("parallel", "arbitrary")),
        name=name,
    )(x, w, h)


def _rmsnorm_kernel(h_ref, g_ref, o_ref):
    x = h_ref[...]
    y = x * lax.rsqrt(jnp.mean(x * x, axis=-1, keepdims=True) + EPS)
    o_ref[...] = (y * g_ref[...]).astype(o_ref.dtype)


def _rmsnorm(h, g, tm):
    lp, d = h.shape
    return pl.pallas_call(
        _rmsnorm_kernel,
        grid=(lp // tm,),
        in_specs=[pl.BlockSpec((tm, d), lambda i: (i, 0)),
                  pl.BlockSpec((1, d), lambda i: (0, 0))],
        out_specs=pl.BlockSpec((tm, d), lambda i: (i, 0)),
        out_shape=jax.ShapeDtypeStruct((lp, d), BF16),
        compiler_params=_params(("parallel",)),
        name="rmsnorm",
    )(h, g.reshape(1, d))


def _final_norm_kernel(h_ref, hn_ref, g_ref, o_ref):
    x = jnp.concatenate([h_ref[N_META:, :], hn_ref[...]], axis=0)
    y = x * lax.rsqrt(jnp.mean(x * x, axis=-1, keepdims=True) + EPS)
    o_ref[...] = y * g_ref[...]


def _final_norm(h, g, seq, tm):
    lp, d = h.shape
    mb = tm // N_META
    last = lp // N_META - 1
    return pl.pallas_call(
        _final_norm_kernel,
        grid=(-(-seq // tm),),
        in_specs=[pl.BlockSpec((tm, d), lambda i: (i, 0)),
                  pl.BlockSpec((N_META, d), lambda i: (jnp.minimum((i + 1) * mb, last), 0)),
                  pl.BlockSpec((1, d), lambda i: (0, 0))],
        out_specs=pl.BlockSpec((tm, d), lambda i: (i, 0)),
        out_shape=jax.ShapeDtypeStruct((seq, d), F32),
        compiler_params=_params(("parallel",)),
        name="final_norm",
    )(h, h, g.reshape(1, d))


def _mlp_up_kernel(x_ref, xh_ref, wg_ref, wu_ref, cg_ref, cu_ref, o_ref, *, n_sub):
    rs = x_ref.shape[0] // n_sub
    keep = (pl.program_id(0) > 0).astype(BF16)
    wg = wg_ref[...].astype(BF16)
    wu = wu_ref[...].astype(BF16)
    for s in range(n_sub):
        if s == 0:
            xe = jnp.concatenate([xh_ref[...] * keep, x_ref[:rs, :]], axis=0)
        else:
            xe = x_ref[s * rs - HALO:(s + 1) * rs, :]
        ug = _causal_conv3(_dot(xe, wg), cg_ref)
        uu = _causal_conv3(_dot(xe, wu), cu_ref)
        o_ref[s * rs:(s + 1) * rs, :] = (ug * _sigmoid(ug) * uu).astype(o_ref.dtype)


def _mlp_up(xn, w_up, layer, conv_w, tm, tn):
    lp, d = xn.shape
    nj = D_FF // tn
    hb = tm // HALO
    n_sub = 4 if tm % (4 * HALO) == 0 else 1
    return pl.pallas_call(
        functools.partial(_mlp_up_kernel, n_sub=n_sub),
        grid=(lp // tm, nj),
        in_specs=[pl.BlockSpec((tm, d), lambda i, j: (i, 0)),
                  pl.BlockSpec((HALO, d), lambda i, j: (jnp.maximum(i * hb - 1, 0), 0)),
                  pl.BlockSpec((None, d, tn), lambda i, j: (layer, 0, j)),
                  pl.BlockSpec((None, d, tn), lambda i, j: (layer, 0, nj + j)),
                  pl.BlockSpec((3, tn), lambda i, j: (0, j)),
                  pl.BlockSpec((3, tn), lambda i, j: (0, nj + j))],
        out_specs=pl.BlockSpec((tm, tn), lambda i, j: (i, j)),
        out_shape=jax.ShapeDtypeStruct((lp, D_FF), BF16),
        compiler_params=_params(("parallel", "arbitrary")),
        name="mlp_up",
    )(xn, xn, w_up, w_up, conv_w, conv_w)


def _reorder_w_in(w):
    f0 = 3 * FOX_WIDTH
    f1 = f0 + FOX_HEADS
    r0 = f1 + 3 * CONV_CH + 2 * GLA_HEADS * GLA_DK + 2 * GLA_HEADS * GLA_DV
    r1 = r0 + GLA_RANK
    main = jnp.concatenate([w[:, r1:], w[:, :f0], w[:, f1:r0]], axis=1).astype(BF16)
    pad = jnp.zeros((w.shape[0], N_SMALL - FOX_HEADS - GLA_RANK), w.dtype)
    small = jnp.concatenate([w[:, f0:f1], w[:, r0:r1], pad], axis=1).astype(BF16)
    return main, small


def _main_colscale():
    s = jnp.ones((1, N_MAIN), F32)
    s = s.at[:, COL_QA:COL_QA + FOX_WIDTH].set(FOX_DH ** -0.5 * LOG2_E)
    s = s.at[:, COL_QC:COL_QC + GLA_HEADS * GLA_DK].set(GLA_DK ** -0.5)
    return s


def kernel(x, meta_tokens, norm1_g, w_in, fox_b_f, gate_b, conv_w, gla_w_g2, gla_b_g, gla_norm_g,
           w_a_o, w_b_o, w_c_o, w_o, norm2_g, w_up, mlp_conv_w, w_down, final_norm_g):
    assert x.shape[0] == 1 and x.shape[2] == D_MODEL
    depth = w_in.shape[0]
    seq = x.shape[1]
    l_real = N_META + seq
    lp = -(-l_real // ROW_TILE) * ROW_TILE
    tm = ROW_TILE
    blk = ROW_TILE

    h, xn = _embed(x[0], meta_tokens.astype(x.dtype), norm1_g[0], lp, tm)
    colscale = _main_colscale()
    ones_small = jnp.ones((1, N_SMALL), F32)

    for l in range(depth):
        w_main, w_small = _reorder_w_in(w_in[l])
        proj = _matmul(xn, w_main, colscale, BF16, _row_tile(lp, 1664), 1024, "in_proj")
        small = _matmul(xn, w_small, ones_small, F32, tm, N_SMALL, "in_proj_small")

        f_blocks = small[:, :FOX_HEADS].T.reshape(FOX_HEADS, lp // 128, 128).transpose(1, 0, 2)
        c = _fox_cumsum(f_blocks, fox_b_f[l])
        c_blocks = c.transpose(1, 0, 2).reshape(FOX_HEADS, lp // blk, 1, blk)
        oa = _fox_attention(proj, c_blocks, blk, 320, 64)

        wg = gla_w_g2[l].reshape(GLA_RANK, GLA_HEADS, GLA_DK).transpose(1, 0, 2)
        wg_pad = jnp.zeros((GLA_HEADS, N_SMALL, GLA_DK), F32)
        wg_pad = wg_pad.at[:, SMALL_GLR:SMALL_GLR + GLA_RANK, :].set(wg)
        oc = _gla(proj, small, wg_pad, gla_b_g[l].reshape(GLA_HEADS, 1, GLA_DK),
                  gla_norm_g[l].reshape(GLA_HEADS, 1, GLA_DV))

        mix = _merge(proj, oa, oc, gate_b[l], conv_w[l], w_a_o[l].astype(BF16),
                     w_b_o[l].astype(BF16), w_c_o[l].astype(BF16), 320)
        h, xn2 = _resid_matmul(mix, w_o[l].astype(BF16), h, norm2_g[l], BF16, 320, D_MODEL, "out_proj")

        act = _mlp_up(xn2, w_up, l, mlp_conv_w[l], _row_tile(lp, 1664), 512)
        h = _matmul_resid(act, w_down[l].astype(BF16), h, _row_tile(lp, 832, 64), 512, "mlp_down")
        if l + 1 < depth:
            xn = _rmsnorm(h, norm1_g[l + 1], tm)
    return _final_norm(h, final_norm_g, seq, tm)[None]
```

```python
import functools

import jax
import jax.numpy as jnp
from jax import lax
from jax.experimental import pallas as pl
from jax.experimental.pallas import tpu as pltpu

F32 = jnp.float32
BF16 = jnp.bfloat16

D_MODEL = 2048
N_META = 16
EPS = 1e-6
NEG = -1e30
LOG2_E = 1.4426950408889634

FOX_HEADS = 8
FOX_DH = D_MODEL // 16
FOX_WIDTH = FOX_HEADS * FOX_DH
CONV_CH = D_MODEL // 2
GLA_HEADS = 4
GLA_DK = D_MODEL // 16
GLA_DV = D_MODEL // 8
GLA_RANK = 16
GLA_TAU = 16.0
GLA_CHUNK = 128
D_FF = D_MODEL * 11 // 4

COL_GA = 0
COL_GB = COL_GA + D_MODEL
COL_GC = COL_GB + D_MODEL
COL_QA = COL_GC + D_MODEL
COL_KA = COL_QA + FOX_WIDTH
COL_VA = COL_KA + FOX_WIDTH
COL_SCB = COL_VA + FOX_WIDTH
COL_SCC = COL_SCB + CONV_CH
COL_SCH = COL_SCC + CONV_CH
COL_QC = COL_SCH + CONV_CH
COL_KC = COL_QC + GLA_HEADS * GLA_DK
COL_VC = COL_KC + GLA_HEADS * GLA_DK
COL_RC = COL_VC + GLA_HEADS * GLA_DV
N_MAIN = COL_RC + GLA_HEADS * GLA_DV
N_SMALL = 128
SMALL_GLR = FOX_HEADS

ROW_TILE = 640
HALO = 16
VMEM_LIMIT = 56 * 1024 * 1024


def _row_tile(lp, cap, granule=128):
    return max(t for t in range(granule, min(cap, lp) + 1, granule) if lp % t == 0)


def _params(sem, vmem=VMEM_LIMIT):
    return pltpu.CompilerParams(dimension_semantics=sem, vmem_limit_bytes=vmem)


def _sigmoid(x):
    return 1.0 / (1.0 + jnp.exp(-x))


def _log_sigmoid(x):
    return jnp.minimum(x, 0.0) - jnp.log(1.0 + jnp.exp(-jnp.abs(x)))


def _dot(a, b):
    return jnp.dot(a, b, preferred_element_type=F32)


def _dot_nt(a, b):
    return lax.dot_general(a, b, (((1,), (1,)), ((), ())), preferred_element_type=F32)


def _split3(x):
    hi = x.astype(BF16)
    r1 = x - hi.astype(F32)
    mid = r1.astype(BF16)
    lo = (r1 - mid.astype(F32)).astype(BF16)
    return hi, mid, lo


def _embed_kernel(x_ref, xp_ref, meta_ref, g_ref, h_ref, n_ref, *, l_real):
    i = pl.program_id(0)
    tm = h_ref.shape[0]
    top = jnp.where(i == 0, meta_ref[...], xp_ref[...])
    hb = jnp.concatenate([top, x_ref[:tm - N_META, :]], axis=0)
    row = i * tm + lax.broadcasted_iota(jnp.int32, (tm, 1), 0)
    hb = jnp.where(row < l_real, hb, 0.0)
    h_ref[...] = hb
    y = hb * lax.rsqrt(jnp.mean(hb * hb, axis=-1, keepdims=True) + EPS)
    n_ref[...] = (y * g_ref[...]).astype(n_ref.dtype)


def _embed(x2, meta, g, lp, tm):
    seq, d = x2.shape
    last_x_block = (seq - 1) // tm
    mb = tm // N_META
    return pl.pallas_call(
        functools.partial(_embed_kernel, l_real=N_META + seq),
        grid=(lp // tm,),
        in_specs=[pl.BlockSpec((tm, d), lambda i: (jnp.minimum(i, last_x_block), 0)),
                  pl.BlockSpec((N_META, d), lambda i: (jnp.maximum(i * mb - 1, 0), 0)),
                  pl.BlockSpec((N_META, d), lambda i: (0, 0)),
                  pl.BlockSpec((1, d), lambda i: (0, 0))],
        out_specs=[pl.BlockSpec((tm, d), lambda i: (i, 0)),
                   pl.BlockSpec((tm, d), lambda i: (i, 0))],
        out_shape=[jax.ShapeDtypeStruct((lp, d), F32),
                   jax.ShapeDtypeStruct((lp, d), BF16)],
        compiler_params=_params(("parallel",)),
        name="embed",
    )(x2, x2, meta, g.reshape(1, d))


def _matmul_kernel(x_ref, w_ref, s_ref, o_ref):
    o_ref[...] = (_dot(x_ref[...], w_ref[...]) * s_ref[...]).astype(o_ref.dtype)


def _matmul(x, w, layer, colscale, out_dtype, tm, tn, name):
    lp, k = x.shape
    n = w.shape[2]
    return pl.pallas_call(
        _matmul_kernel,
        grid=(lp // tm, n // tn),
        in_specs=[pl.BlockSpec((tm, k), lambda i, j: (i, 0)),
                  pl.BlockSpec((None, k, tn), lambda i, j: (layer, 0, j)),
                  pl.BlockSpec((1, tn), lambda i, j: (0, j))],
        out_specs=pl.BlockSpec((tm, tn), lambda i, j: (i, j)),
        out_shape=jax.ShapeDtypeStruct((lp, n), out_dtype),
        compiler_params=_params(("parallel", "arbitrary")),
        name=name,
    )(x, w, colscale)


def _fox_cumsum_kernel(f_ref, b_ref, c_ref):
    nblk = f_ref.shape[0]
    row = lax.broadcasted_iota(jnp.int32, (128, 128), 0)
    col = lax.broadcasted_iota(jnp.int32, (128, 128), 1)
    triu = jnp.where(row <= col, 1.0, 0.0).astype(BF16)

    def body(j, carry):
        logf = _log_sigmoid(f_ref[j] + b_ref[...]) * LOG2_E
        hi, mid, lo = _split3(logf)
        cs = _dot(hi, triu) + _dot(mid, triu) + _dot(lo, triu) + carry
        c_ref[j] = cs
        return cs[:, 127:128]

    lax.fori_loop(0, nblk, body, jnp.zeros((f_ref.shape[1], 1), F32))


def _fox_cumsum(f_blocks, b_f):
    nblk, nh, _ = f_blocks.shape
    return pl.pallas_call(
        _fox_cumsum_kernel,
        out_shape=jax.ShapeDtypeStruct((nblk, nh, 128), F32),
        name="fox_cumsum",
    )(f_blocks, b_f.reshape(nh, 1))


def _lane_tile(x, n):
    return jnp.concatenate([x] * n, axis=1)


def _fox_kernel(q_ref, k_ref, v_ref, c_ref, o_ref, vx_sc, s_sc, p_sc, a_sc, m_sc, acc_sc,
                *, blk, part, sub):
    i = pl.program_id(1)
    dh = q_ref.shape[1]

    @pl.when(i == 0)
    def _():
        vx_sc[:, :dh] = v_ref[...]
        vx_sc[:, dh:] = jnp.ones((vx_sc.shape[0], dh), BF16)

    m_sc[...] = jnp.full(m_sc.shape, NEG, F32)
    acc_sc[...] = jnp.zeros(acc_sc.shape, F32)

    def rows_of(j):
        return pl.ds(pl.multiple_of(j * blk, blk), blk)

    def logits(j, slot):
        kj = k_ref[rows_of(j), :]
        for a in range(0, blk, part):
            s_sc[slot, a:a + part, :] = _dot_nt(q_ref[a:a + part, :], kj)

    def softmax(j, slot, masked):
        cj = c_ref[j]

        def biased(r):
            s = s_sc[slot, r:r + sub, :] - cj
            if masked:
                qpos = r + lax.broadcasted_iota(jnp.int32, (sub, blk), 0)
                kpos = lax.broadcasted_iota(jnp.int32, (sub, blk), 1)
                s = jnp.where(kpos <= qpos, s, NEG)
            return s

        for r in range(0, blk, sub):
            m_prev = m_sc[r:r + sub, :]
            m_new = jnp.maximum(m_prev, jnp.max(biased(r), axis=-1, keepdims=True))
            a_sc[slot, r:r + sub, :] = jnp.exp2(m_prev - m_new)
            m_sc[r:r + sub, :] = m_new
        for r in range(0, blk, sub):
            m_new = _lane_tile(m_sc[r:r + sub, :], blk // 128)
            p_sc[slot, r:r + sub, :] = jnp.exp2(biased(r) - m_new).astype(BF16)

    def values(j, slot):
        vj = vx_sc[rows_of(j), :]
        for a in range(0, blk, part):
            acc_sc[a:a + part, :] = (acc_sc[a:a + part, :] * _lane_tile(a_sc[slot, a:a + part, :], 2)
                                     + _dot(p_sc[slot, a:a + part, :], vj))

    def step(j, slot, masked, prefetch, prev):
        if prefetch:
            logits(j + 1, 1 - slot)
        if prev:
            values(j - 1, 1 - slot)
        softmax(j, slot, masked)

    def step_pair(t, _):
        step(2 * t + 1, 1, masked=False, prefetch=True, prev=True)
        step(2 * t + 2, 0, masked=False, prefetch=True, prev=True)
        return 0

    logits(0, 0)

    @pl.when(i == 0)
    def _():
        softmax(0, 0, masked=True)
        values(0, 0)

    @pl.when(i > 0)
    def _():
        step(0, 0, masked=False, prefetch=True, prev=False)
        lax.fori_loop(0, (i - 1) // 2, step_pair, 0)

        @pl.when(i % 2 == 0)
        def _():
            step(i - 1, 1, masked=False, prefetch=True, prev=True)
            step(i, 0, masked=True, prefetch=False, prev=True)
            values(i, 0)

        @pl.when(i % 2 == 1)
        def _():
            step(i, 1, masked=True, prefetch=False, prev=True)
            values(i, 1)

    o_ref[...] = (acc_sc[:, :dh] / acc_sc[:, dh:]).astype(o_ref.dtype)


def _fox_attention(proj, c_blocks, blk, part, sub):
    lp = proj.shape[0]
    nb = lp // blk
    dh = FOX_DH
    kern = functools.partial(_fox_kernel, blk=blk, part=part, sub=sub)
    return pl.pallas_call(
        kern,
        grid=(FOX_HEADS, nb),
        in_specs=[pl.BlockSpec((blk, dh), lambda h, i: (i, COL_QA // dh + h)),
                  pl.BlockSpec((lp, dh), lambda h, i: (0, COL_KA // dh + h)),
                  pl.BlockSpec((lp, dh), lambda h, i: (0, COL_VA // dh + h)),
                  pl.BlockSpec((None, nb, 1, blk), lambda h, i: (h, 0, 0, 0))],
        out_specs=pl.BlockSpec((blk, dh), lambda h, i: (i, h)),
        out_shape=jax.ShapeDtypeStruct((lp, FOX_WIDTH), BF16),
        scratch_shapes=[pltpu.VMEM((lp, 2 * dh), BF16),
                        pltpu.VMEM((2, blk, blk), F32),
                        pltpu.VMEM((2, blk, blk), BF16),
                        pltpu.VMEM((2, blk, 128), F32),
                        pltpu.VMEM((blk, 128), F32),
                        pltpu.VMEM((blk, 2 * dh), F32)],
        compiler_params=_params(("arbitrary", "arbitrary")),
        name="fox_attention",
    )(proj, proj, proj, c_blocks)


def _segment_reference_rows(b, n, sub8):
    c = b.shape[0]
    half = n // 2
    if n >= 16:
        pieces = [jnp.broadcast_to(b[m * n + half - 1:m * n + half, :], (n, b.shape[1]))
                  for m in range(c // n)]
        return pieces[0] if len(pieces) == 1 else jnp.concatenate(pieces, axis=0)
    b3 = b.reshape(c // 8, 8, b.shape[1])

    def bc(r):
        return jnp.broadcast_to(b3[:, r:r + 1, :], b3.shape).reshape(b.shape)

    out = bc(8 - n + half - 1)
    for start in range(8 - 2 * n, -1, -n):
        out = jnp.where(sub8 < start + n, bc(start + half - 1), out)
    return out


def _gla_kernel(q_ref, k_ref, v_ref, r_ref, sm_ref, wg_ref, bg_ref, ng_ref, o_ref, st_ref):
    c, dk, dv = GLA_CHUNK, GLA_DK, GLA_DV

    @pl.when(pl.program_id(0) == 0)
    def _():
        st_ref[...] = jnp.zeros(st_ref.shape, F32)

    t_idx = lax.broadcasted_iota(jnp.int32, (c, c), 0)
    s_idx = lax.broadcasted_iota(jnp.int32, (c, c), 1)
    tril = jnp.where(s_idx <= t_idx, 1.0, 0.0).astype(BF16)
    sub8 = t_idx & 7
    sm = sm_ref[...].astype(BF16)

    for h in range(GLA_HEADS):
        q = q_ref[:, h * dk:(h + 1) * dk].astype(F32)
        k = k_ref[:, h * dk:(h + 1) * dk].astype(F32)
        v = v_ref[:, h * dv:(h + 1) * dv]
        z = _dot(sm, wg_ref[h].astype(BF16)) + bg_ref[h]
        g = _log_sigmoid(z) * (1.0 / GLA_TAU)
        hi, mid, lo = _split3(g)
        b = _dot(tril, hi) + _dot(tril, mid) + _dot(tril, lo)

        att = jnp.where(t_idx == s_idx, _dot_nt(q.astype(BF16), k.astype(BF16)), 0.0)
        n = c
        while n >= 2:
            upper = (t_idx & (n - 1)) >= n // 2
            ref = _segment_reference_rows(b, n, sub8)
            qs = jnp.where(upper, q * jnp.exp(jnp.minimum(b - ref, 0.0)), 0.0).astype(BF16)
            ks = jnp.where(upper, 0.0, k * jnp.exp(jnp.minimum(ref - b, 0.0))).astype(BF16)
            a = _dot_nt(qs, ks)
            if n < c:
                shift = n.bit_length() - 1
                a = jnp.where((t_idx >> shift) == (s_idx >> shift), a, 0.0)
            att = att + a
            n //= 2

        st = st_ref[h]
        o = _dot(att.astype(BF16), v) + _dot_nt((q * jnp.exp(b)).astype(BF16), st.astype(BF16))
        b_last = b[c - 1:c, :]
        ke = (k * jnp.exp(b_last - b)).astype(BF16)
        st_ref[h] = st * jnp.exp(b_last) + lax.dot_general(
            v, ke, (((0,), (0,)), ((), ())), preferred_element_type=F32)

        on = o * lax.rsqrt(jnp.mean(o * o, axis=-1, keepdims=True) + EPS) * ng_ref[h]
        r = r_ref[:, h * dv:(h + 1) * dv].astype(F32)
        o_ref[:, h * dv:(h + 1) * dv] = (r * _sigmoid(r) * on).astype(o_ref.dtype)


def _gla(proj, small, wg_pad, b_g, norm_g):
    lp = proj.shape[0]
    c, dk, dv, nh = GLA_CHUNK, GLA_DK, GLA_DV, GLA_HEADS
    assert dk == c
    return pl.pallas_call(
        _gla_kernel,
        grid=(lp // c,),
        in_specs=[pl.BlockSpec((c, nh * dk), lambda i: (i, COL_QC // (nh * dk))),
                  pl.BlockSpec((c, nh * dk), lambda i: (i, COL_KC // (nh * dk))),
                  pl.BlockSpec((c, nh * dv), lambda i: (i, COL_VC // (nh * dv))),
                  pl.BlockSpec((c, nh * dv), lambda i: (i, COL_RC // (nh * dv))),
                  pl.BlockSpec((c, N_SMALL), lambda i: (i, 0)),
                  pl.BlockSpec((nh, N_SMALL, dk), lambda i: (0, 0, 0)),
                  pl.BlockSpec((nh, 1, dk), lambda i: (0, 0, 0)),
                  pl.BlockSpec((nh, 1, dv), lambda i: (0, 0, 0))],
        out_specs=pl.BlockSpec((c, nh * dv), lambda i: (i, 0)),
        out_shape=jax.ShapeDtypeStruct((lp, nh * dv), BF16),
        scratch_shapes=[pltpu.VMEM((nh, dv, dk), F32)],
        compiler_params=_params(("arbitrary",)),
        name="gla",
    )(proj, proj, proj, proj, small, wg_pad, b_g, norm_g)


def _causal_conv3(ext, w_ref):
    m1 = pltpu.roll(ext, 1, 0)
    m2 = pltpu.roll(ext, 2, 0)
    return (ext[HALO:] * w_ref[2:3, :] + m1[HALO:] * w_ref[1:2, :] + m2[HALO:] * w_ref[0:1, :])


def _merge_kernel(oa_ref, scb_ref, scc_ref, sch_ref, scc_h_ref, sch_h_ref, oc_ref,
                  ga_ref, gb_ref, gc_ref, gbias_ref, cw_ref, wa_ref, wb_ref, wc_ref, o_ref):
    keep = (pl.program_id(0) > 0).astype(F32)
    p_halo = scc_h_ref[...].astype(F32) * sch_h_ref[...].astype(F32) * keep
    p = scc_ref[...].astype(F32) * sch_ref[...].astype(F32)
    conv = _causal_conv3(jnp.concatenate([p_halo, p], axis=0), cw_ref)
    ub = (scb_ref[...].astype(F32) * conv).astype(BF16)
    ya = _dot(oa_ref[...], wa_ref[...])
    yb = _dot(ub, wb_ref[...])
    yc = _dot(oc_ref[...], wc_ref[...])
    mix = (_sigmoid(ga_ref[...].astype(F32) + gbias_ref[0:1, :]) * ya
           + _sigmoid(gb_ref[...].astype(F32) + gbias_ref[1:2, :]) * yb
           + _sigmoid(gc_ref[...].astype(F32) + gbias_ref[2:3, :]) * yc)
    o_ref[...] = mix.astype(o_ref.dtype)


def _merge(proj, oa, oc, gate_b, conv_w, w_a_o, w_b_o, w_c_o, layer, tm):
    lp = proj.shape[0]
    d = D_MODEL
    ch = CONV_CH
    hb = tm // HALO

    def col(c0, width):
        return lambda i: (i, c0 // width)

    def halo(c0):
        return lambda i: (jnp.maximum(i * hb - 1, 0), c0 // ch)

    def const(shape):
        return pl.BlockSpec(shape, lambda i: (0, 0), pipeline_mode=pl.Buffered(1))

    def weight(k):
        return pl.BlockSpec((None, k, d), lambda i: (layer, 0, 0), pipeline_mode=pl.Buffered(1))

    return pl.pallas_call(
        _merge_kernel,
        grid=(lp // tm,),
        in_specs=[pl.BlockSpec((tm, FOX_WIDTH), lambda i: (i, 0)),
                  pl.BlockSpec((tm, ch), col(COL_SCB, ch)),
                  pl.BlockSpec((tm, ch), col(COL_SCC, ch)),
                  pl.BlockSpec((tm, ch), col(COL_SCH, ch)),
                  pl.BlockSpec((HALO, ch), halo(COL_SCC)),
                  pl.BlockSpec((HALO, ch), halo(COL_SCH)),
                  pl.BlockSpec((tm, GLA_HEADS * GLA_DV), lambda i: (i, 0)),
                  pl.BlockSpec((tm, d), col(COL_GA, d)),
                  pl.BlockSpec((tm, d), col(COL_GB, d)),
                  pl.BlockSpec((tm, d), col(COL_GC, d)),
                  const((3, d)),
                  const((3, ch)),
                  weight(FOX_WIDTH),
                  weight(ch),
                  weight(GLA_HEADS * GLA_DV)],
        out_specs=pl.BlockSpec((tm, d), lambda i: (i, 0)),
        out_shape=jax.ShapeDtypeStruct((lp, d), BF16),
        compiler_params=_params(("parallel",)),
        name="merge",
    )(oa, proj, proj, proj, proj, proj, oc, proj, proj, proj,
      gate_b.reshape(3, d), conv_w, w_a_o, w_b_o, w_c_o)


def _resid_matmul_kernel(x_ref, w_ref, h_ref, g_ref, ho_ref, no_ref, acc_ref):
    kk = pl.program_id(1)

    @pl.when(kk == 0)
    def _():
        acc_ref[...] = jnp.zeros(acc_ref.shape, F32)

    acc_ref[...] += _dot(x_ref[...], w_ref[...])

    @pl.when(kk == pl.num_programs(1) - 1)
    def _():
        hn = h_ref[...] + acc_ref[...]
        ho_ref[...] = hn
        y = hn * lax.rsqrt(jnp.mean(hn * hn, axis=-1, keepdims=True) + EPS)
        no_ref[...] = (y * g_ref[...]).astype(no_ref.dtype)


def _resid_matmul(x, w, layer, h, g, norm_dtype, tm, tk, name):
    lp, k = x.shape
    d = w.shape[2]
    return pl.pallas_call(
        _resid_matmul_kernel,
        grid=(lp // tm, k // tk),
        in_specs=[pl.BlockSpec((tm, tk), lambda i, kk: (i, kk)),
                  pl.BlockSpec((None, tk, d), lambda i, kk: (layer, kk, 0)),
                  pl.BlockSpec((tm, d), lambda i, kk: (i, 0)),
                  pl.BlockSpec((1, d), lambda i, kk: (0, 0))],
        out_specs=[pl.BlockSpec((tm, d), lambda i, kk: (i, 0)),
                   pl.BlockSpec((tm, d), lambda i, kk: (i, 0))],
        out_shape=[jax.ShapeDtypeStruct((lp, d), F32),
                   jax.ShapeDtypeStruct((lp, d), norm_dtype)],
        scratch_shapes=[pltpu.VMEM((tm, d), F32)],
        compiler_params=_params(("parallel", "arbitrary")),
        name=name,
    )(x, w, h, g.reshape(1, d))


def _matmul_resid_kernel(x_ref, w_ref, h_ref, o_ref):
    o_ref[...] = h_ref[...] + _dot(x_ref[...], w_ref[...])


def _matmul_resid(x, w, layer, h, tm, tn, name):
    lp, k = x.shape
    n = w.shape[2]
    return pl.pallas_call(
        _matmul_resid_kernel,
        grid=(lp // tm, n // tn),
        in_specs=[pl.BlockSpec((tm, k), lambda i, j: (i, 0)),
                  pl.BlockSpec((None, k, tn), lambda i, j: (layer, 0, j)),
                  pl.BlockSpec((tm, tn), lambda i, j: (i, j))],
        out_specs=pl.BlockSpec((tm, tn), lambda i, j: (i, j)),
        out_shape=jax.ShapeDtypeStruct((lp, n), F32),
        compiler_params=_params(("parallel", "arbitrary")),
        name=name,
    )(x, w, h)


def _rmsnorm_kernel(h_ref, g_ref, o_ref):
    x = h_ref[...]
    y = x * lax.rsqrt(jnp.mean(x * x, axis=-1, keepdims=True) + EPS)
    o_ref[...] = (y * g_ref[...]).astype(o_ref.dtype)


def _rmsnorm(h, g, tm):
    lp, d = h.shape
    return pl.pallas_call(
        _rmsnorm_kernel,
        grid=(lp // tm,),
        in_specs=[pl.BlockSpec((tm, d), lambda i: (i, 0)),
                  pl.BlockSpec((1, d), lambda i: (0, 0))],
        out_specs=pl.BlockSpec((tm, d), lambda i: (i, 0)),
        out_shape=jax.ShapeDtypeStruct((lp, d), BF16),
        compiler_params=_params(("parallel",)),
        name="rmsnorm",
    )(h, g.reshape(1, d))


def _final_norm_kernel(h_ref, hn_ref, g_ref, o_ref):
    x = jnp.concatenate([h_ref[N_META:, :], hn_ref[...]], axis=0)
    y = x * lax.rsqrt(jnp.mean(x * x, axis=-1, keepdims=True) + EPS)
    o_ref[...] = y * g_ref[...]


def _final_norm(h, g, seq, tm):
    lp, d = h.shape
    mb = tm // N_META
    last = lp // N_META - 1
    return pl.pallas_call(
        _final_norm_kernel,
        grid=(-(-seq // tm),),
        in_specs=[pl.BlockSpec((tm, d), lambda i: (i, 0)),
                  pl.BlockSpec((N_META, d), lambda i: (jnp.minimum((i + 1) * mb, last), 0)),
                  pl.BlockSpec((1, d), lambda i: (0, 0))],
        out_specs=pl.BlockSpec((tm, d), lambda i: (i, 0)),
        out_shape=jax.ShapeDtypeStruct((seq, d), F32),
        compiler_params=_params(("parallel",)),
        name="final_norm",
    )(h, h, g.reshape(1, d))


def _mlp_up_kernel(x_ref, xh_ref, wg_ref, wu_ref, cg_ref, cu_ref, o_ref, *, n_sub):
    rs = x_ref.shape[0] // n_sub
    keep = (pl.program_id(0) > 0).astype(BF16)
    wg = wg_ref[...].astype(BF16)
    wu = wu_ref[...].astype(BF16)
    for s in range(n_sub):
        if s == 0:
            xe = jnp.concatenate([xh_ref[...] * keep, x_ref[:rs, :]], axis=0)
        else:
            xe = x_ref[s * rs - HALO:(s + 1) * rs, :]
        ug = _causal_conv3(_dot(xe, wg), cg_ref)
        uu = _causal_conv3(_dot(xe, wu), cu_ref)
        o_ref[s * rs:(s + 1) * rs, :] = (ug * _sigmoid(ug) * uu).astype(o_ref.dtype)


def _mlp_up(xn, w_up, layer, conv_w, tm, tn):
    lp, d = xn.shape
    nj = D_FF // tn
    hb = tm // HALO
    n_sub = 4 if tm % (4 * HALO) == 0 else 1
    return pl.pallas_call(
        functools.partial(_mlp_up_kernel, n_sub=n_sub),
        grid=(lp // tm, nj),
        in_specs=[pl.BlockSpec((tm, d), lambda i, j: (i, 0)),
                  pl.BlockSpec((HALO, d), lambda i, j: (jnp.maximum(i * hb - 1, 0), 0)),
                  pl.BlockSpec((None, d, tn), lambda i, j: (layer, 0, j)),
                  pl.BlockSpec((None, d, tn), lambda i, j: (layer, 0, nj + j)),
                  pl.BlockSpec((3, tn), lambda i, j: (0, j)),
                  pl.BlockSpec((3, tn), lambda i, j: (0, nj + j))],
        out_specs=pl.BlockSpec((tm, tn), lambda i, j: (i, j)),
        out_shape=jax.ShapeDtypeStruct((lp, D_FF), BF16),
        compiler_params=_params(("parallel", "arbitrary")),
        name="mlp_up",
    )(xn, xn, w_up, w_up, conv_w, conv_w)


SRC_FA = 3 * FOX_WIDTH
SRC_MID = SRC_FA + FOX_HEADS
SRC_GLR = SRC_MID + 3 * CONV_CH + 2 * GLA_HEADS * GLA_DK + 2 * GLA_HEADS * GLA_DV
SRC_GATE = SRC_GLR + GLA_RANK
REORDER_TW = 512
N_GATE_TILES = 3 * D_MODEL // REORDER_TW
N_FOX_TILES = 3 * FOX_WIDTH // REORDER_TW


def _reorder_kernel(a_ref, b_ref, o_ref):
    j = pl.program_id(2)
    tw = o_ref.shape[1]

    def shifted(lanes):
        window = jnp.concatenate([a_ref[...], b_ref[...]], axis=1)
        return window[:, lanes:lanes + tw]

    @pl.when(j < N_GATE_TILES)
    def _():
        o_ref[...] = shifted(SRC_GATE % 128).astype(o_ref.dtype)

    @pl.when((j >= N_GATE_TILES) & (j < N_GATE_TILES + N_FOX_TILES))
    def _():
        o_ref[...] = a_ref[...].astype(o_ref.dtype)

    @pl.when(j >= N_GATE_TILES + N_FOX_TILES)
    def _():
        o_ref[...] = shifted(SRC_MID % 128).astype(o_ref.dtype)


def _reorder_w_in(w_in, tr):
    depth, d, _ = w_in.shape
    tw = REORDER_TW
    assert (SRC_GATE - SRC_GATE % 128) % tw == 0 and (SRC_MID - SRC_MID % 128) % tw == 0
    gate0 = (SRC_GATE - SRC_GATE % 128) // tw
    rest0 = N_GATE_TILES

    def win(j):
        return jnp.where(j < N_GATE_TILES, gate0 + j, j - rest0)

    return pl.pallas_call(
        _reorder_kernel,
        grid=(depth, d // tr, N_MAIN // tw),
        in_specs=[pl.BlockSpec((None, tr, tw), lambda l, r, j: (l, r, win(j))),
                  pl.BlockSpec((None, tr, 128), lambda l, r, j: (l, r, (win(j) + 1) * (tw // 128)))],
        out_specs=pl.BlockSpec((None, tr, tw), lambda l, r, j: (l, r, j)),
        out_shape=jax.ShapeDtypeStruct((depth, d, N_MAIN), BF16),
        compiler_params=_params(("parallel", "parallel", "arbitrary")),
        name="reorder_w_in",
    )(w_in, w_in)


def _small_w_in(w_in):
    pad = jnp.zeros(w_in.shape[:2] + (N_SMALL - FOX_HEADS - GLA_RANK,), w_in.dtype)
    return jnp.concatenate([w_in[:, :, SRC_FA:SRC_MID], w_in[:, :, SRC_GLR:SRC_GATE], pad],
                           axis=2).astype(BF16)


def _main_colscale():
    s = jnp.ones((1, N_MAIN), F32)
    s = s.at[:, COL_QA:COL_QA + FOX_WIDTH].set(FOX_DH ** -0.5 * LOG2_E)
    s = s.at[:, COL_QC:COL_QC + GLA_HEADS * GLA_DK].set(GLA_DK ** -0.5)
    return s


def kernel(x, meta_tokens, norm1_g, w_in, fox_b_f, gate_b, conv_w, gla_w_g2, gla_b_g, gla_norm_g,
           w_a_o, w_b_o, w_c_o, w_o, norm2_g, w_up, mlp_conv_w, w_down, final_norm_g):
    assert x.shape[0] == 1 and x.shape[2] == D_MODEL
    depth = w_in.shape[0]
    seq = x.shape[1]
    l_real = N_META + seq
    lp = -(-l_real // ROW_TILE) * ROW_TILE
    tm = ROW_TILE
    blk = ROW_TILE

    h, xn = _embed(x[0], meta_tokens.astype(x.dtype), norm1_g[0], lp, tm)
    colscale = _main_colscale()
    ones_small = jnp.ones((1, N_SMALL), F32)
    w_main = _reorder_w_in(w_in, 1024)
    w_small = _small_w_in(w_in)
    w_a_o, w_b_o, w_c_o, w_o, w_down = (w.astype(BF16) for w in (w_a_o, w_b_o, w_c_o, w_o, w_down))

    for l in range(depth):
        proj = _matmul(xn, w_main, l, colscale, BF16, _row_tile(lp, 1664), 1024, "in_proj")
        small = _matmul(xn, w_small, l, ones_small, F32, tm, N_SMALL, "in_proj_small")

        f_blocks = small[:, :FOX_HEADS].T.reshape(FOX_HEADS, lp // 128, 128).transpose(1, 0, 2)
        c = _fox_cumsum(f_blocks, fox_b_f[l])
        c_blocks = c.transpose(1, 0, 2).reshape(FOX_HEADS, lp // blk, 1, blk)
        oa = _fox_attention(proj, c_blocks, blk, 320, 64)

        wg = gla_w_g2[l].reshape(GLA_RANK, GLA_HEADS, GLA_DK).transpose(1, 0, 2)
        wg_pad = jnp.zeros((GLA_HEADS, N_SMALL, GLA_DK), F32)
        wg_pad = wg_pad.at[:, SMALL_GLR:SMALL_GLR + GLA_RANK, :].set(wg)
        oc = _gla(proj, small, wg_pad, gla_b_g[l].reshape(GLA_HEADS, 1, GLA_DK),
                  gla_norm_g[l].reshape(GLA_HEADS, 1, GLA_DV))

        mix = _merge(proj, oa, oc, gate_b[l], conv_w[l], w_a_o, w_b_o, w_c_o, l, 320)
        h, xn2 = _resid_matmul(mix, w_o, l, h, norm2_g[l], BF16, 320, D_MODEL, "out_proj")

        act = _mlp_up(xn2, w_up, l, mlp_conv_w[l], _row_tile(lp, 1664), 512)
        h = _matmul_resid(act, w_down, l, h, _row_tile(lp, 832, 64), 512, "mlp_down")
        if l + 1 < depth:
            xn = _rmsnorm(h, norm1_g[l + 1], tm)
    return _final_norm(h, final_norm_g, seq, tm)[None]
```

```python
import functools

import jax
import jax.numpy as jnp
from jax import lax
from jax.experimental import pallas as pl
from jax.experimental.pallas import tpu as pltpu

F32 = jnp.float32
BF16 = jnp.bfloat16

D_MODEL = 2048
N_META = 16
EPS = 1e-6
NEG = -1e30
LOG2_E = 1.4426950408889634

FOX_HEADS = 8
FOX_DH = D_MODEL // 16
FOX_WIDTH = FOX_HEADS * FOX_DH
CONV_CH = D_MODEL // 2
GLA_HEADS = 4
GLA_DK = D_MODEL // 16
GLA_DV = D_MODEL // 8
GLA_RANK = 16
GLA_TAU = 16.0
GLA_CHUNK = 128
D_FF = D_MODEL * 11 // 4

COL_GA = 0
COL_GB = COL_GA + D_MODEL
COL_GC = COL_GB + D_MODEL
COL_QA = COL_GC + D_MODEL
COL_KA = COL_QA + FOX_WIDTH
COL_VA = COL_KA + FOX_WIDTH
COL_SCB = COL_VA + FOX_WIDTH
COL_SCC = COL_SCB + CONV_CH
COL_SCH = COL_SCC + CONV_CH
COL_QC = COL_SCH + CONV_CH
COL_KC = COL_QC + GLA_HEADS * GLA_DK
COL_VC = COL_KC + GLA_HEADS * GLA_DK
COL_RC = COL_VC + GLA_HEADS * GLA_DV
N_MAIN = COL_RC + GLA_HEADS * GLA_DV
N_SMALL = 128
SMALL_GLR = FOX_HEADS

ROW_TILE = 640
HALO = 16
VMEM_LIMIT = 56 * 1024 * 1024


def _row_tile(lp, cap, granule=128):
    return max(t for t in range(granule, min(cap, lp) + 1, granule) if lp % t == 0)


def _params(sem, vmem=VMEM_LIMIT):
    return pltpu.CompilerParams(dimension_semantics=sem, vmem_limit_bytes=vmem)


def _sigmoid(x):
    return 1.0 / (1.0 + jnp.exp(-x))


def _log_sigmoid(x):
    return jnp.minimum(x, 0.0) - jnp.log(1.0 + jnp.exp(-jnp.abs(x)))


def _dot(a, b):
    return jnp.dot(a, b, preferred_element_type=F32)


def _dot_nt(a, b):
    return lax.dot_general(a, b, (((1,), (1,)), ((), ())), preferred_element_type=F32)


def _split3(x):
    hi = x.astype(BF16)
    r1 = x - hi.astype(F32)
    mid = r1.astype(BF16)
    lo = (r1 - mid.astype(F32)).astype(BF16)
    return hi, mid, lo


def _embed_kernel(x_ref, xp_ref, meta_ref, g_ref, h_ref, n_ref, *, l_real):
    i = pl.program_id(0)
    tm = h_ref.shape[0]
    top = jnp.where(i == 0, meta_ref[...], xp_ref[...])
    hb = jnp.concatenate([top, x_ref[:tm - N_META, :]], axis=0)
    row = i * tm + lax.broadcasted_iota(jnp.int32, (tm, 1), 0)
    hb = jnp.where(row < l_real, hb, 0.0)
    h_ref[...] = hb
    y = hb * lax.rsqrt(jnp.mean(hb * hb, axis=-1, keepdims=True) + EPS)
    n_ref[...] = (y * g_ref[...]).astype(n_ref.dtype)


def _embed(x2, meta, g, lp, tm):
    seq, d = x2.shape
    last_x_block = (seq - 1) // tm
    mb = tm // N_META
    return pl.pallas_call(
        functools.partial(_embed_kernel, l_real=N_META + seq),
        grid=(lp // tm,),
        in_specs=[pl.BlockSpec((tm, d), lambda i: (jnp.minimum(i, last_x_block), 0)),
                  pl.BlockSpec((N_META, d), lambda i: (jnp.maximum(i * mb - 1, 0), 0)),
                  pl.BlockSpec((N_META, d), lambda i: (0, 0)),
                  pl.BlockSpec((1, d), lambda i: (0, 0))],
        out_specs=[pl.BlockSpec((tm, d), lambda i: (i, 0)),
                   pl.BlockSpec((tm, d), lambda i: (i, 0))],
        out_shape=[jax.ShapeDtypeStruct((lp, d), F32),
                   jax.ShapeDtypeStruct((lp, d), BF16)],
        compiler_params=_params(("parallel",)),
        name="embed",
    )(x2, x2, meta, g.reshape(1, d))


def _fox_cumsum_kernel(f_ref, b_ref, c_ref):
    nblk = f_ref.shape[0]
    row = lax.broadcasted_iota(jnp.int32, (128, 128), 0)
    col = lax.broadcasted_iota(jnp.int32, (128, 128), 1)
    triu = jnp.where(row <= col, 1.0, 0.0).astype(BF16)

    def body(j, carry):
        logf = _log_sigmoid(f_ref[j] + b_ref[...]) * LOG2_E
        hi, mid, lo = _split3(logf)
        cs = _dot(hi, triu) + _dot(mid, triu) + _dot(lo, triu) + carry
        c_ref[j] = cs
        return cs[:, 127:128]

    lax.fori_loop(0, nblk, body, jnp.zeros((f_ref.shape[1], 1), F32))


def _fox_cumsum(f_blocks, b_f):
    nblk, nh, _ = f_blocks.shape
    return pl.pallas_call(
        _fox_cumsum_kernel,
        out_shape=jax.ShapeDtypeStruct((nblk, nh, 128), F32),
        name="fox_cumsum",
    )(f_blocks, b_f.reshape(nh, 1))


def _lane_tile(x, n):
    return jnp.concatenate([x] * n, axis=1)


def _fox_kernel(q_ref, k_ref, v_ref, c_ref, o_ref, vx_sc, s_sc, p_sc, a_sc, m_sc, acc_sc,
                *, blk, part, sub):
    i = pl.program_id(1)
    dh = q_ref.shape[1]

    @pl.when(i == 0)
    def _():
        vx_sc[:, :dh] = v_ref[...]
        vx_sc[:, dh:] = jnp.ones((vx_sc.shape[0], dh), BF16)

    m_sc[...] = jnp.full(m_sc.shape, NEG, F32)
    acc_sc[...] = jnp.zeros(acc_sc.shape, F32)

    def rows_of(j):
        return pl.ds(pl.multiple_of(j * blk, blk), blk)

    def logits(j, slot):
        kj = k_ref[rows_of(j), :]
        for a in range(0, blk, part):
            s_sc[slot, a:a + part, :] = _dot_nt(q_ref[a:a + part, :], kj)

    def softmax(j, slot, masked):
        cj = c_ref[j]

        def biased(r):
            s = s_sc[slot, r:r + sub, :] - cj
            if masked:
                qpos = r + lax.broadcasted_iota(jnp.int32, (sub, blk), 0)
                kpos = lax.broadcasted_iota(jnp.int32, (sub, blk), 1)
                s = jnp.where(kpos <= qpos, s, NEG)
            return s

        for r in range(0, blk, sub):
            m_prev = m_sc[r:r + sub, :]
            m_new = jnp.maximum(m_prev, jnp.max(biased(r), axis=-1, keepdims=True))
            a_sc[slot, r:r + sub, :] = jnp.exp2(m_prev - m_new)
            m_sc[r:r + sub, :] = m_new
        for r in range(0, blk, sub):
            m_new = _lane_tile(m_sc[r:r + sub, :], blk // 128)
            p_sc[slot, r:r + sub, :] = jnp.exp2(biased(r) - m_new).astype(BF16)

    def values(j, slot):
        vj = vx_sc[rows_of(j), :]
        for a in range(0, blk, part):
            acc_sc[a:a + part, :] = (acc_sc[a:a + part, :] * _lane_tile(a_sc[slot, a:a + part, :], 2)
                                     + _dot(p_sc[slot, a:a + part, :], vj))

    def step(j, slot, masked, prefetch, prev):
        if prefetch:
            logits(j + 1, 1 - slot)
        if prev:
            values(j - 1, 1 - slot)
        softmax(j, slot, masked)

    def step_pair(t, _):
        step(2 * t + 1, 1, masked=False, prefetch=True, prev=True)
        step(2 * t + 2, 0, masked=False, prefetch=True, prev=True)
        return 0

    logits(0, 0)

    @pl.when(i == 0)
    def _():
        softmax(0, 0, masked=True)
        values(0, 0)

    @pl.when(i > 0)
    def _():
        step(0, 0, masked=False, prefetch=True, prev=False)
        lax.fori_loop(0, (i - 1) // 2, step_pair, 0)

        @pl.when(i % 2 == 0)
        def _():
            step(i - 1, 1, masked=False, prefetch=True, prev=True)
            step(i, 0, masked=True, prefetch=False, prev=True)
            values(i, 0)

        @pl.when(i % 2 == 1)
        def _():
            step(i, 1, masked=True, prefetch=False, prev=True)
            values(i, 1)

    o_ref[...] = (acc_sc[:, :dh] / acc_sc[:, dh:]).astype(o_ref.dtype)


def _fox_attention(proj, c_blocks, blk, part, sub):
    lp = proj.shape[0]
    nb = lp // blk
    dh = FOX_DH
    kern = functools.partial(_fox_kernel, blk=blk, part=part, sub=sub)
    return pl.pallas_call(
        kern,
        grid=(FOX_HEADS, nb),
        in_specs=[pl.BlockSpec((blk, dh), lambda h, i: (i, COL_QA // dh + h)),
                  pl.BlockSpec((lp, dh), lambda h, i: (0, COL_KA // dh + h)),
                  pl.BlockSpec((lp, dh), lambda h, i: (0, COL_VA // dh + h)),
                  pl.BlockSpec((None, nb, 1, blk), lambda h, i: (h, 0, 0, 0))],
        out_specs=pl.BlockSpec((blk, dh), lambda h, i: (i, h)),
        out_shape=jax.ShapeDtypeStruct((lp, FOX_WIDTH), BF16),
        scratch_shapes=[pltpu.VMEM((lp, 2 * dh), BF16),
                        pltpu.VMEM((2, blk, blk), F32),
                        pltpu.VMEM((2, blk, blk), BF16),
                        pltpu.VMEM((2, blk, 128), F32),
                        pltpu.VMEM((blk, 128), F32),
                        pltpu.VMEM((blk, 2 * dh), F32)],
        compiler_params=_params(("arbitrary", "arbitrary")),
        name="fox_attention",
    )(proj, proj, proj, c_blocks)


def _segment_reference_rows(b, n, sub8):
    c = b.shape[0]
    half = n // 2
    if n >= 16:
        pieces = [jnp.broadcast_to(b[m * n + half - 1:m * n + half, :], (n, b.shape[1]))
                  for m in range(c // n)]
        return pieces[0] if len(pieces) == 1 else jnp.concatenate(pieces, axis=0)
    b3 = b.reshape(c // 8, 8, b.shape[1])

    def bc(r):
        return jnp.broadcast_to(b3[:, r:r + 1, :], b3.shape).reshape(b.shape)

    out = bc(8 - n + half - 1)
    for start in range(8 - 2 * n, -1, -n):
        out = jnp.where(sub8 < start + n, bc(start + half - 1), out)
    return out


def _gla_kernel(q_ref, k_ref, v_ref, r_ref, sm_ref, wg_ref, bg_ref, ng_ref, o_ref, st_ref):
    c, dk, dv = GLA_CHUNK, GLA_DK, GLA_DV

    @pl.when(pl.program_id(0) == 0)
    def _():
        st_ref[...] = jnp.zeros(st_ref.shape, F32)

    t_idx = lax.broadcasted_iota(jnp.int32, (c, c), 0)
    s_idx = lax.broadcasted_iota(jnp.int32, (c, c), 1)
    tril = jnp.where(s_idx <= t_idx, 1.0, 0.0).astype(BF16)
    sub8 = t_idx & 7
    sm = sm_ref[...].astype(BF16)

    for h in range(GLA_HEADS):
        q = q_ref[:, h * dk:(h + 1) * dk].astype(F32)
        k = k_ref[:, h * dk:(h + 1) * dk].astype(F32)
        v = v_ref[:, h * dv:(h + 1) * dv]
        z = _dot(sm, wg_ref[h].astype(BF16)) + bg_ref[h]
        g = _log_sigmoid(z) * (1.0 / GLA_TAU)
        hi, mid, lo = _split3(g)
        b = _dot(tril, hi) + _dot(tril, mid) + _dot(tril, lo)

        att = jnp.where(t_idx == s_idx, _dot_nt(q.astype(BF16), k.astype(BF16)), 0.0)
        n = c
        while n >= 2:
            upper = (t_idx & (n - 1)) >= n // 2
            ref = _segment_reference_rows(b, n, sub8)
            qs = jnp.where(upper, q * jnp.exp(jnp.minimum(b - ref, 0.0)), 0.0).astype(BF16)
            ks = jnp.where(upper, 0.0, k * jnp.exp(jnp.minimum(ref - b, 0.0))).astype(BF16)
            a = _dot_nt(qs, ks)
            if n < c:
                shift = n.bit_length() - 1
                a = jnp.where((t_idx >> shift) == (s_idx >> shift), a, 0.0)
            att = att + a
            n //= 2

        st = st_ref[h]
        o = _dot(att.astype(BF16), v) + _dot_nt((q * jnp.exp(b)).astype(BF16), st.astype(BF16))
        b_last = b[c - 1:c, :]
        ke = (k * jnp.exp(b_last - b)).astype(BF16)
        st_ref[h] = st * jnp.exp(b_last) + lax.dot_general(
            v, ke, (((0,), (0,)), ((), ())), preferred_element_type=F32)

        on = o * lax.rsqrt(jnp.mean(o * o, axis=-1, keepdims=True) + EPS) * ng_ref[h]
        r = r_ref[:, h * dv:(h + 1) * dv].astype(F32)
        o_ref[:, h * dv:(h + 1) * dv] = (r * _sigmoid(r) * on).astype(o_ref.dtype)


def _gla(proj, small, wg_pad, b_g, norm_g):
    lp = proj.shape[0]
    c, dk, dv, nh = GLA_CHUNK, GLA_DK, GLA_DV, GLA_HEADS
    assert dk == c
    return pl.pallas_call(
        _gla_kernel,
        grid=(lp // c,),
        in_specs=[pl.BlockSpec((c, nh * dk), lambda i: (i, COL_QC // (nh * dk))),
                  pl.BlockSpec((c, nh * dk), lambda i: (i, COL_KC // (nh * dk))),
                  pl.BlockSpec((c, nh * dv), lambda i: (i, COL_VC // (nh * dv))),
                  pl.BlockSpec((c, nh * dv), lambda i: (i, COL_RC // (nh * dv))),
                  pl.BlockSpec((c, N_SMALL), lambda i: (i, 0)),
                  pl.BlockSpec((nh, N_SMALL, dk), lambda i: (0, 0, 0)),
                  pl.BlockSpec((nh, 1, dk), lambda i: (0, 0, 0)),
                  pl.BlockSpec((nh, 1, dv), lambda i: (0, 0, 0))],
        out_specs=pl.BlockSpec((c, nh * dv), lambda i: (i, 0)),
        out_shape=jax.ShapeDtypeStruct((lp, nh * dv), BF16),
        scratch_shapes=[pltpu.VMEM((nh, dv, dk), F32)],
        compiler_params=_params(("arbitrary",)),
        name="gla",
    )(proj, proj, proj, proj, small, wg_pad, b_g, norm_g)


def _causal_conv3(ext, w_ref):
    m1 = pltpu.roll(ext, 1, 0)
    m2 = pltpu.roll(ext, 2, 0)
    return (ext[HALO:] * w_ref[2:3, :] + m1[HALO:] * w_ref[1:2, :] + m2[HALO:] * w_ref[0:1, :])


def _merge_kernel(oa_ref, scb_ref, scc_ref, sch_ref, scc_h_ref, sch_h_ref, oc_ref,
                  ga_ref, gb_ref, gc_ref, gbias_ref, cw_ref, wa_ref, wb_ref, wc_ref, o_ref):
    keep = (pl.program_id(0) > 0).astype(F32)
    p_halo = scc_h_ref[...].astype(F32) * sch_h_ref[...].astype(F32) * keep
    p = scc_ref[...].astype(F32) * sch_ref[...].astype(F32)
    conv = _causal_conv3(jnp.concatenate([p_halo, p], axis=0), cw_ref)
    ub = (scb_ref[...].astype(F32) * conv).astype(BF16)
    ya = _dot(oa_ref[...], wa_ref[...])
    yb = _dot(ub, wb_ref[...])
    yc = _dot(oc_ref[...], wc_ref[...])
    mix = (_sigmoid(ga_ref[...].astype(F32) + gbias_ref[0:1, :]) * ya
           + _sigmoid(gb_ref[...].astype(F32) + gbias_ref[1:2, :]) * yb
           + _sigmoid(gc_ref[...].astype(F32) + gbias_ref[2:3, :]) * yc)
    o_ref[...] = mix.astype(o_ref.dtype)


def _merge(proj, oa, oc, gate_b, conv_w, w_a_o, w_b_o, w_c_o, layer, tm):
    lp = proj.shape[0]
    d = D_MODEL
    ch = CONV_CH
    hb = tm // HALO

    def col(c0, width):
        return lambda i: (i, c0 // width)

    def halo(c0):
        return lambda i: (jnp.maximum(i * hb - 1, 0), c0 // ch)

    def const(shape):
        return pl.BlockSpec(shape, lambda i: (0, 0), pipeline_mode=pl.Buffered(1))

    def weight(k):
        return pl.BlockSpec((None, k, d), lambda i: (layer, 0, 0), pipeline_mode=pl.Buffered(1))

    return pl.pallas_call(
        _merge_kernel,
        grid=(lp // tm,),
        in_specs=[pl.BlockSpec((tm, FOX_WIDTH), lambda i: (i, 0)),
                  pl.BlockSpec((tm, ch), col(COL_SCB, ch)),
                  pl.BlockSpec((tm, ch), col(COL_SCC, ch)),
                  pl.BlockSpec((tm, ch), col(COL_SCH, ch)),
                  pl.BlockSpec((HALO, ch), halo(COL_SCC)),
                  pl.BlockSpec((HALO, ch), halo(COL_SCH)),
                  pl.BlockSpec((tm, GLA_HEADS * GLA_DV), lambda i: (i, 0)),
                  pl.BlockSpec((tm, d), col(COL_GA, d)),
                  pl.BlockSpec((tm, d), col(COL_GB, d)),
                  pl.BlockSpec((tm, d), col(COL_GC, d)),
                  const((3, d)),
                  const((3, ch)),
                  weight(FOX_WIDTH),
                  weight(ch),
                  weight(GLA_HEADS * GLA_DV)],
        out_specs=pl.BlockSpec((tm, d), lambda i: (i, 0)),
        out_shape=jax.ShapeDtypeStruct((lp, d), BF16),
        compiler_params=_params(("parallel",)),
        name="merge",
    )(oa, proj, proj, proj, proj, proj, oc, proj, proj, proj,
      gate_b.reshape(3, d), conv_w, w_a_o, w_b_o, w_c_o)


def _resid_matmul_kernel(x_ref, w_ref, h_ref, g_ref, ho_ref, no_ref, acc_ref):
    kk = pl.program_id(1)

    @pl.when(kk == 0)
    def _():
        acc_ref[...] = jnp.zeros(acc_ref.shape, F32)

    acc_ref[...] += _dot(x_ref[...], w_ref[...])

    @pl.when(kk == pl.num_programs(1) - 1)
    def _():
        hn = h_ref[...] + acc_ref[...]
        ho_ref[...] = hn
        y = hn * lax.rsqrt(jnp.mean(hn * hn, axis=-1, keepdims=True) + EPS)
        no_ref[...] = (y * g_ref[...]).astype(no_ref.dtype)


def _resid_matmul(x, w, layer, h, g, norm_dtype, tm, tk, name):
    lp, k = x.shape
    d = w.shape[2]
    return pl.pallas_call(
        _resid_matmul_kernel,
        grid=(lp // tm, k // tk),
        in_specs=[pl.BlockSpec((tm, tk), lambda i, kk: (i, kk)),
                  pl.BlockSpec((None, tk, d), lambda i, kk: (layer, kk, 0)),
                  pl.BlockSpec((tm, d), lambda i, kk: (i, 0)),
                  pl.BlockSpec((1, d), lambda i, kk: (0, 0))],
        out_specs=[pl.BlockSpec((tm, d), lambda i, kk: (i, 0)),
                   pl.BlockSpec((tm, d), lambda i, kk: (i, 0))],
        out_shape=[jax.ShapeDtypeStruct((lp, d), F32),
                   jax.ShapeDtypeStruct((lp, d), norm_dtype)],
        scratch_shapes=[pltpu.VMEM((tm, d), F32)],
        compiler_params=_params(("parallel", "arbitrary")),
        name=name,
    )(x, w, h, g.reshape(1, d))


def _matmul_resid_kernel(x_ref, w_ref, h_ref, o_ref):
    o_ref[...] = h_ref[...] + _dot(x_ref[...], w_ref[...])


def _matmul_resid(x, w, layer, h, tm, tn, name):
    lp, k = x.shape
    n = w.shape[2]
    return pl.pallas_call(
        _matmul_resid_kernel,
        grid=(lp // tm, n // tn),
        in_specs=[pl.BlockSpec((tm, k), lambda i, j: (i, 0)),
                  pl.BlockSpec((None, k, tn), lambda i, j: (layer, 0, j)),
                  pl.BlockSpec((tm, tn), lambda i, j: (i, j))],
        out_specs=pl.BlockSpec((tm, tn), lambda i, j: (i, j)),
        out_shape=jax.ShapeDtypeStruct((lp, n), F32),
        compiler_params=_params(("parallel", "arbitrary")),
        name=name,
    )(x, w, h)


def _rmsnorm_kernel(h_ref, g_ref, o_ref):
    x = h_ref[...]
    y = x * lax.rsqrt(jnp.mean(x * x, axis=-1, keepdims=True) + EPS)
    o_ref[...] = (y * g_ref[...]).astype(o_ref.dtype)


def _rmsnorm(h, g, tm):
    lp, d = h.shape
    return pl.pallas_call(
        _rmsnorm_kernel,
        grid=(lp // tm,),
        in_specs=[pl.BlockSpec((tm, d), lambda i: (i, 0)),
                  pl.BlockSpec((1, d), lambda i: (0, 0))],
        out_specs=pl.BlockSpec((tm, d), lambda i: (i, 0)),
        out_shape=jax.ShapeDtypeStruct((lp, d), BF16),
        compiler_params=_params(("parallel",)),
        name="rmsnorm",
    )(h, g.reshape(1, d))


def _final_norm_kernel(h_ref, hn_ref, g_ref, o_ref):
    x = jnp.concatenate([h_ref[N_META:, :], hn_ref[...]], axis=0)
    y = x * lax.rsqrt(jnp.mean(x * x, axis=-1, keepdims=True) + EPS)
    o_ref[...] = y * g_ref[...]


def _final_norm(h, g, seq, tm):
    lp, d = h.shape
    mb = tm // N_META
    last = lp // N_META - 1
    return pl.pallas_call(
        _final_norm_kernel,
        grid=(-(-seq // tm),),
        in_specs=[pl.BlockSpec((tm, d), lambda i: (i, 0)),
                  pl.BlockSpec((N_META, d), lambda i: (jnp.minimum((i + 1) * mb, last), 0)),
                  pl.BlockSpec((1, d), lambda i: (0, 0))],
        out_specs=pl.BlockSpec((tm, d), lambda i: (i, 0)),
        out_shape=jax.ShapeDtypeStruct((seq, d), F32),
        compiler_params=_params(("parallel",)),
        name="final_norm",
    )(h, h, g.reshape(1, d))


def _mlp_up_kernel(x_ref, xh_ref, wg_ref, wu_ref, cg_ref, cu_ref, o_ref, *, n_sub):
    rs = x_ref.shape[0] // n_sub
    keep = (pl.program_id(0) > 0).astype(BF16)
    wg = wg_ref[...].astype(BF16)
    wu = wu_ref[...].astype(BF16)
    for s in range(n_sub):
        if s == 0:
            xe = jnp.concatenate([xh_ref[...] * keep, x_ref[:rs, :]], axis=0)
        else:
            xe = x_ref[s * rs - HALO:(s + 1) * rs, :]
        ug = _causal_conv3(_dot(xe, wg), cg_ref)
        uu = _causal_conv3(_dot(xe, wu), cu_ref)
        o_ref[s * rs:(s + 1) * rs, :] = (ug * _sigmoid(ug) * uu).astype(o_ref.dtype)


def _mlp_up(xn, w_up, layer, conv_w, tm, tn):
    lp, d = xn.shape
    nj = D_FF // tn
    hb = tm // HALO
    n_sub = 4 if tm % (4 * HALO) == 0 else 1
    return pl.pallas_call(
        functools.partial(_mlp_up_kernel, n_sub=n_sub),
        grid=(lp // tm, nj),
        in_specs=[pl.BlockSpec((tm, d), lambda i, j: (i, 0)),
                  pl.BlockSpec((HALO, d), lambda i, j: (jnp.maximum(i * hb - 1, 0), 0)),
                  pl.BlockSpec((None, d, tn), lambda i, j: (layer, 0, j)),
                  pl.BlockSpec((None, d, tn), lambda i, j: (layer, 0, nj + j)),
                  pl.BlockSpec((3, tn), lambda i, j: (0, j)),
                  pl.BlockSpec((3, tn), lambda i, j: (0, nj + j))],
        out_specs=pl.BlockSpec((tm, tn), lambda i, j: (i, j)),
        out_shape=jax.ShapeDtypeStruct((lp, D_FF), BF16),
        compiler_params=_params(("parallel", "arbitrary")),
        name="mlp_up",
    )(xn, xn, w_up, w_up, conv_w, conv_w)


SRC_FA = 3 * FOX_WIDTH
SRC_MID = SRC_FA + FOX_HEADS
SRC_GLR = SRC_MID + 3 * CONV_CH + 2 * GLA_HEADS * GLA_DK + 2 * GLA_HEADS * GLA_DV
SRC_GATE = SRC_GLR + GLA_RANK
def _in_proj_kernel(x_ref, w_ref, s_ref, o_ref):
    w = w_ref[...].astype(BF16)
    o_ref[...] = (_dot_nt(x_ref[...], w) * s_ref[...]).astype(o_ref.dtype)


def _in_proj(xn, w_t, layer, colscale, tm, tn):
    lp, k = xn.shape
    assert COL_QA % tn == 0 and COL_SCB % tn == 0 and N_MAIN % tn == 0

    assert SRC_GATE % 8 == 0 and SRC_MID % 8 == 0 and tn % 8 == 0

    def src_row(j):
        c8 = j * (tn // 8)
        r8 = jnp.where(c8 < COL_QA // 8, SRC_GATE // 8 + c8,
                       jnp.where(c8 < COL_SCB // 8, c8 - COL_QA // 8,
                                 (SRC_MID - COL_SCB) // 8 + c8))
        return r8 * 8

    return pl.pallas_call(
        _in_proj_kernel,
        grid=(lp // tm, N_MAIN // tn),
        in_specs=[pl.BlockSpec((tm, k), lambda i, j: (i, 0)),
                  pl.BlockSpec((None, pl.Element(tn), pl.Element(k)),
                               lambda i, j: (layer, src_row(j), 0)),
                  pl.BlockSpec((1, tn), lambda i, j: (0, j))],
        out_specs=pl.BlockSpec((tm, tn), lambda i, j: (i, j)),
        out_shape=jax.ShapeDtypeStruct((lp, N_MAIN), BF16),
        compiler_params=_params(("parallel", "arbitrary")),
        name="in_proj",
    )(xn, w_t, colscale)


def _in_proj_small_kernel(x_ref, w_ref, o_ref):
    o_ref[...] = _dot_nt(x_ref[...], w_ref[...])


def _in_proj_small(xn, w_small_t, layer, tm):
    lp, k = xn.shape
    return pl.pallas_call(
        _in_proj_small_kernel,
        grid=(lp // tm,),
        in_specs=[pl.BlockSpec((tm, k), lambda i: (i, 0)),
                  pl.BlockSpec((None, N_SMALL, k), lambda i: (layer, 0, 0))],
        out_specs=pl.BlockSpec((tm, N_SMALL), lambda i: (i, 0)),
        out_shape=jax.ShapeDtypeStruct((lp, N_SMALL), F32),
        compiler_params=_params(("parallel",)),
        name="in_proj_small",
    )(xn, w_small_t)


def _small_w_in(w_t):
    pad = jnp.zeros((w_t.shape[0], N_SMALL - FOX_HEADS - GLA_RANK, w_t.shape[2]), w_t.dtype)
    return jnp.concatenate([w_t[:, SRC_FA:SRC_MID], w_t[:, SRC_GLR:SRC_GATE], pad],
                           axis=1).astype(BF16)


def _main_colscale():
    s = jnp.ones((1, N_MAIN), F32)
    s = s.at[:, COL_QA:COL_QA + FOX_WIDTH].set(FOX_DH ** -0.5 * LOG2_E)
    s = s.at[:, COL_QC:COL_QC + GLA_HEADS * GLA_DK].set(GLA_DK ** -0.5)
    return s


def kernel(x, meta_tokens, norm1_g, w_in, fox_b_f, gate_b, conv_w, gla_w_g2, gla_b_g, gla_norm_g,
           w_a_o, w_b_o, w_c_o, w_o, norm2_g, w_up, mlp_conv_w, w_down, final_norm_g):
    assert x.shape[0] == 1 and x.shape[2] == D_MODEL
    depth = w_in.shape[0]
    seq = x.shape[1]
    l_real = N_META + seq
    lp = -(-l_real // ROW_TILE) * ROW_TILE
    tm = ROW_TILE
    blk = ROW_TILE

    h, xn = _embed(x[0], meta_tokens.astype(x.dtype), norm1_g[0], lp, tm)
    colscale = _main_colscale()
    w_t = jnp.swapaxes(w_in, 1, 2)
    w_small_t = _small_w_in(w_t)
    w_a_o, w_b_o, w_c_o, w_o, w_down = (w.astype(BF16) for w in (w_a_o, w_b_o, w_c_o, w_o, w_down))

    for l in range(depth):
        proj = _in_proj(xn, w_t, l, colscale, _row_tile(lp, 1664), 512)
        small = _in_proj_small(xn, w_small_t, l, tm)

        f_blocks = small[:, :FOX_HEADS].T.reshape(FOX_HEADS, lp // 128, 128).transpose(1, 0, 2)
        c = _fox_cumsum(f_blocks, fox_b_f[l])
        c_blocks = c.transpose(1, 0, 2).reshape(FOX_HEADS, lp // blk, 1, blk)
        oa = _fox_attention(proj, c_blocks, blk, 320, 64)

        wg = gla_w_g2[l].reshape(GLA_RANK, GLA_HEADS, GLA_DK).transpose(1, 0, 2)
        wg_pad = jnp.zeros((GLA_HEADS, N_SMALL, GLA_DK), F32)
        wg_pad = wg_pad.at[:, SMALL_GLR:SMALL_GLR + GLA_RANK, :].set(wg)
        oc = _gla(proj, small, wg_pad, gla_b_g[l].reshape(GLA_HEADS, 1, GLA_DK),
                  gla_norm_g[l].reshape(GLA_HEADS, 1, GLA_DV))

        mix = _merge(proj, oa, oc, gate_b[l], conv_w[l], w_a_o, w_b_o, w_c_o, l, 320)
        h, xn2 = _resid_matmul(mix, w_o, l, h, norm2_g[l], BF16, 320, D_MODEL, "out_proj")

        act = _mlp_up(xn2, w_up, l, mlp_conv_w[l], _row_tile(lp, 1664), 512)
        h = _matmul_resid(act, w_down, l, h, _row_tile(lp, 832, 64), 512, "mlp_down")
        if l + 1 < depth:
            xn = _rmsnorm(h, norm1_g[l + 1], tm)
    return _final_norm(h, final_norm_g, seq, tm)[None]
```

```python
import functools

import jax
import jax.numpy as jnp
from jax import lax
from jax.experimental import pallas as pl
from jax.experimental.pallas import tpu as pltpu

F32 = jnp.float32
BF16 = jnp.bfloat16

D_MODEL = 2048
N_META = 16
EPS = 1e-6
NEG = -1e30
LOG2_E = 1.4426950408889634

FOX_HEADS = 8
FOX_DH = D_MODEL // 16
FOX_WIDTH = FOX_HEADS * FOX_DH
CONV_CH = D_MODEL // 2
GLA_HEADS = 4
GLA_DK = D_MODEL // 16
GLA_DV = D_MODEL // 8
GLA_RANK = 16
GLA_TAU = 16.0
GLA_CHUNK = 128
D_FF = D_MODEL * 11 // 4

COL_GA = 0
COL_GB = COL_GA + D_MODEL
COL_GC = COL_GB + D_MODEL
COL_QA = COL_GC + D_MODEL
COL_KA = COL_QA + FOX_WIDTH
COL_VA = COL_KA + FOX_WIDTH
COL_SCB = COL_VA + FOX_WIDTH
COL_SCC = COL_SCB + CONV_CH
COL_SCH = COL_SCC + CONV_CH
COL_QC = COL_SCH + CONV_CH
COL_KC = COL_QC + GLA_HEADS * GLA_DK
COL_VC = COL_KC + GLA_HEADS * GLA_DK
COL_RC = COL_VC + GLA_HEADS * GLA_DV
N_MAIN = COL_RC + GLA_HEADS * GLA_DV
N_SMALL = 128
SMALL_GLR = FOX_HEADS

ROW_TILE = 640
HALO = 16
VMEM_LIMIT = 56 * 1024 * 1024


def _row_tile(lp, cap, granule=128):
    return max(t for t in range(granule, min(cap, lp) + 1, granule) if lp % t == 0)


def _params(sem, vmem=VMEM_LIMIT):
    return pltpu.CompilerParams(dimension_semantics=sem, vmem_limit_bytes=vmem)


def _sigmoid(x):
    return 1.0 / (1.0 + jnp.exp(-x))


def _log_sigmoid(x):
    return jnp.minimum(x, 0.0) - jnp.log(1.0 + jnp.exp(-jnp.abs(x)))


def _dot(a, b):
    return jnp.dot(a, b, preferred_element_type=F32)


def _dot_nt(a, b):
    return lax.dot_general(a, b, (((1,), (1,)), ((), ())), preferred_element_type=F32)


def _split3(x):
    hi = x.astype(BF16)
    r1 = x - hi.astype(F32)
    mid = r1.astype(BF16)
    lo = (r1 - mid.astype(F32)).astype(BF16)
    return hi, mid, lo


def _embed_kernel(x_ref, xp_ref, meta_ref, g_ref, h_ref, n_ref, *, l_real):
    i = pl.program_id(0)
    tm = h_ref.shape[0]
    top = jnp.where(i == 0, meta_ref[...], xp_ref[...])
    hb = jnp.concatenate([top, x_ref[:tm - N_META, :]], axis=0)
    row = i * tm + lax.broadcasted_iota(jnp.int32, (tm, 1), 0)
    hb = jnp.where(row < l_real, hb, 0.0)
    h_ref[...] = hb
    y = hb * lax.rsqrt(jnp.mean(hb * hb, axis=-1, keepdims=True) + EPS)
    n_ref[...] = (y * g_ref[...]).astype(n_ref.dtype)


def _embed(x2, meta, g, lp, tm):
    seq, d = x2.shape
    last_x_block = (seq - 1) // tm
    mb = tm // N_META
    return pl.pallas_call(
        functools.partial(_embed_kernel, l_real=N_META + seq),
        grid=(lp // tm,),
        in_specs=[pl.BlockSpec((tm, d), lambda i: (jnp.minimum(i, last_x_block), 0)),
                  pl.BlockSpec((N_META, d), lambda i: (jnp.maximum(i * mb - 1, 0), 0)),
                  pl.BlockSpec((N_META, d), lambda i: (0, 0)),
                  pl.BlockSpec((1, d), lambda i: (0, 0))],
        out_specs=[pl.BlockSpec((tm, d), lambda i: (i, 0)),
                   pl.BlockSpec((tm, d), lambda i: (i, 0))],
        out_shape=[jax.ShapeDtypeStruct((lp, d), F32),
                   jax.ShapeDtypeStruct((lp, d), BF16)],
        compiler_params=_params(("parallel",)),
        name="embed",
    )(x2, x2, meta, g.reshape(1, d))


def _fox_cumsum_kernel(f_ref, b_ref, c_ref):
    nblk = f_ref.shape[0]
    row = lax.broadcasted_iota(jnp.int32, (128, 128), 0)
    col = lax.broadcasted_iota(jnp.int32, (128, 128), 1)
    triu = jnp.where(row <= col, 1.0, 0.0).astype(BF16)

    def body(j, carry):
        logf = _log_sigmoid(f_ref[j] + b_ref[...]) * LOG2_E
        hi, mid, lo = _split3(logf)
        cs = _dot(hi, triu) + _dot(mid, triu) + _dot(lo, triu) + carry
        c_ref[j] = cs
        return cs[:, 127:128]

    lax.fori_loop(0, nblk, body, jnp.zeros((f_ref.shape[1], 1), F32))


def _fox_cumsum(f_blocks, b_f):
    nblk, nh, _ = f_blocks.shape
    return pl.pallas_call(
        _fox_cumsum_kernel,
        out_shape=jax.ShapeDtypeStruct((nblk, nh, 128), F32),
        name="fox_cumsum",
    )(f_blocks, b_f.reshape(nh, 1))


def _lane_tile(x, n):
    return jnp.concatenate([x] * n, axis=1)


def _fox_kernel(q_ref, k_ref, v_ref, c_ref, o_ref, vx_sc, s_sc, p_sc, a_sc, m_sc, acc_sc,
                *, blk, part, sub):
    i = pl.program_id(1)
    dh = q_ref.shape[1]

    @pl.when(i == 0)
    def _():
        vx_sc[:, :dh] = v_ref[...]
        vx_sc[:, dh:] = jnp.ones((vx_sc.shape[0], dh), BF16)

    m_sc[...] = jnp.full(m_sc.shape, NEG, F32)
    acc_sc[...] = jnp.zeros(acc_sc.shape, F32)

    def rows_of(j):
        return pl.ds(pl.multiple_of(j * blk, blk), blk)

    def logits(j, slot):
        kj = k_ref[rows_of(j), :]
        for a in range(0, blk, part):
            s_sc[slot, a:a + part, :] = _dot_nt(q_ref[a:a + part, :], kj)

    def softmax(j, slot, masked):
        cj = c_ref[j]

        def biased(r):
            s = s_sc[slot, r:r + sub, :] - cj
            if masked:
                qpos = r + lax.broadcasted_iota(jnp.int32, (sub, blk), 0)
                kpos = lax.broadcasted_iota(jnp.int32, (sub, blk), 1)
                s = jnp.where(kpos <= qpos, s, NEG)
            return s

        for r in range(0, blk, sub):
            m_prev = m_sc[r:r + sub, :]
            m_new = jnp.maximum(m_prev, jnp.max(biased(r), axis=-1, keepdims=True))
            a_sc[slot, r:r + sub, :] = jnp.exp2(m_prev - m_new)
            m_sc[r:r + sub, :] = m_new
        for r in range(0, blk, sub):
            m_new = _lane_tile(m_sc[r:r + sub, :], blk // 128)
            p_sc[slot, r:r + sub, :] = jnp.exp2(biased(r) - m_new).astype(BF16)

    def values(j, slot):
        vj = vx_sc[rows_of(j), :]
        for a in range(0, blk, part):
            acc_sc[a:a + part, :] = (acc_sc[a:a + part, :] * _lane_tile(a_sc[slot, a:a + part, :], 2)
                                     + _dot(p_sc[slot, a:a + part, :], vj))

    def step(j, slot, masked, prefetch, prev):
        if prefetch:
            logits(j + 1, 1 - slot)
        if prev:
            values(j - 1, 1 - slot)
        softmax(j, slot, masked)

    def step_pair(t, _):
        step(2 * t + 1, 1, masked=False, prefetch=True, prev=True)
        step(2 * t + 2, 0, masked=False, prefetch=True, prev=True)
        return 0

    logits(0, 0)

    @pl.when(i == 0)
    def _():
        softmax(0, 0, masked=True)
        values(0, 0)

    @pl.when(i > 0)
    def _():
        step(0, 0, masked=False, prefetch=True, prev=False)
        lax.fori_loop(0, (i - 1) // 2, step_pair, 0)

        @pl.when(i % 2 == 0)
        def _():
            step(i - 1, 1, masked=False, prefetch=True, prev=True)
            step(i, 0, masked=True, prefetch=False, prev=True)
            values(i, 0)

        @pl.when(i % 2 == 1)
        def _():
            step(i, 1, masked=True, prefetch=False, prev=True)
            values(i, 1)

    o_ref[...] = (acc_sc[:, :dh] / acc_sc[:, dh:]).astype(o_ref.dtype)


def _fox_attention(proj, c_blocks, blk, part, sub):
    lp = proj.shape[0]
    nb = lp // blk
    dh = FOX_DH
    kern = functools.partial(_fox_kernel, blk=blk, part=part, sub=sub)
    return pl.pallas_call(
        kern,
        grid=(FOX_HEADS, nb),
        in_specs=[pl.BlockSpec((blk, dh), lambda h, i: (i, COL_QA // dh + h)),
                  pl.BlockSpec((lp, dh), lambda h, i: (0, COL_KA // dh + h)),
                  pl.BlockSpec((lp, dh), lambda h, i: (0, COL_VA // dh + h)),
                  pl.BlockSpec((None, nb, 1, blk), lambda h, i: (h, 0, 0, 0))],
        out_specs=pl.BlockSpec((blk, dh), lambda h, i: (i, h)),
        out_shape=jax.ShapeDtypeStruct((lp, FOX_WIDTH), BF16),
        scratch_shapes=[pltpu.VMEM((lp, 2 * dh), BF16),
                        pltpu.VMEM((2, blk, blk), F32),
                        pltpu.VMEM((2, blk, blk), BF16),
                        pltpu.VMEM((2, blk, 128), F32),
                        pltpu.VMEM((blk, 128), F32),
                        pltpu.VMEM((blk, 2 * dh), F32)],
        compiler_params=_params(("arbitrary", "arbitrary")),
        name="fox_attention",
    )(proj, proj, proj, c_blocks)


def _segment_reference_rows(b, n, sub8):
    c = b.shape[0]
    half = n // 2
    if n >= 16:
        pieces = [jnp.broadcast_to(b[m * n + half - 1:m * n + half, :], (n, b.shape[1]))
                  for m in range(c // n)]
        return pieces[0] if len(pieces) == 1 else jnp.concatenate(pieces, axis=0)
    b3 = b.reshape(c // 8, 8, b.shape[1])

    def bc(r):
        return jnp.broadcast_to(b3[:, r:r + 1, :], b3.shape).reshape(b.shape)

    out = bc(8 - n + half - 1)
    for start in range(8 - 2 * n, -1, -n):
        out = jnp.where(sub8 < start + n, bc(start + half - 1), out)
    return out


def _gla_kernel(q_ref, k_ref, v_ref, r_ref, sm_ref, wg_ref, bg_ref, ng_ref, o_ref, st_ref):
    c, dk, dv = GLA_CHUNK, GLA_DK, GLA_DV

    @pl.when(pl.program_id(0) == 0)
    def _():
        st_ref[...] = jnp.zeros(st_ref.shape, F32)

    t_idx = lax.broadcasted_iota(jnp.int32, (c, c), 0)
    s_idx = lax.broadcasted_iota(jnp.int32, (c, c), 1)
    tril = jnp.where(s_idx <= t_idx, 1.0, 0.0).astype(BF16)
    sub8 = t_idx & 7
    sm = sm_ref[...].astype(BF16)

    for h in range(GLA_HEADS):
        q = q_ref[:, h * dk:(h + 1) * dk].astype(F32)
        k = k_ref[:, h * dk:(h + 1) * dk].astype(F32)
        v = v_ref[:, h * dv:(h + 1) * dv]
        z = _dot(sm, wg_ref[h].astype(BF16)) + bg_ref[h]
        g = _log_sigmoid(z) * (1.0 / GLA_TAU)
        hi, mid, lo = _split3(g)
        b = _dot(tril, hi) + _dot(tril, mid) + _dot(tril, lo)

        att = jnp.where(t_idx == s_idx, _dot_nt(q.astype(BF16), k.astype(BF16)), 0.0)
        n = c
        while n >= 2:
            upper = (t_idx & (n - 1)) >= n // 2
            ref = _segment_reference_rows(b, n, sub8)
            qs = jnp.where(upper, q * jnp.exp(jnp.minimum(b - ref, 0.0)), 0.0).astype(BF16)
            ks = jnp.where(upper, 0.0, k * jnp.exp(jnp.minimum(ref - b, 0.0))).astype(BF16)
            a = _dot_nt(qs, ks)
            if n < c:
                shift = n.bit_length() - 1
                a = jnp.where((t_idx >> shift) == (s_idx >> shift), a, 0.0)
            att = att + a
            n //= 2

        st = st_ref[h]
        o = _dot(att.astype(BF16), v) + _dot_nt((q * jnp.exp(b)).astype(BF16), st.astype(BF16))
        b_last = b[c - 1:c, :]
        ke = (k * jnp.exp(b_last - b)).astype(BF16)
        st_ref[h] = st * jnp.exp(b_last) + lax.dot_general(
            v, ke, (((0,), (0,)), ((), ())), preferred_element_type=F32)

        on = o * lax.rsqrt(jnp.mean(o * o, axis=-1, keepdims=True) + EPS) * ng_ref[h]
        r = r_ref[:, h * dv:(h + 1) * dv].astype(F32)
        o_ref[:, h * dv:(h + 1) * dv] = (r * _sigmoid(r) * on).astype(o_ref.dtype)


def _gla(proj, small, wg_pad, b_g, norm_g):
    lp = proj.shape[0]
    c, dk, dv, nh = GLA_CHUNK, GLA_DK, GLA_DV, GLA_HEADS
    assert dk == c
    return pl.pallas_call(
        _gla_kernel,
        grid=(lp // c,),
        in_specs=[pl.BlockSpec((c, nh * dk), lambda i: (i, COL_QC // (nh * dk))),
                  pl.BlockSpec((c, nh * dk), lambda i: (i, COL_KC // (nh * dk))),
                  pl.BlockSpec((c, nh * dv), lambda i: (i, COL_VC // (nh * dv))),
                  pl.BlockSpec((c, nh * dv), lambda i: (i, COL_RC // (nh * dv))),
                  pl.BlockSpec((c, N_SMALL), lambda i: (i, 0)),
                  pl.BlockSpec((nh, N_SMALL, dk), lambda i: (0, 0, 0)),
                  pl.BlockSpec((nh, 1, dk), lambda i: (0, 0, 0)),
                  pl.BlockSpec((nh, 1, dv), lambda i: (0, 0, 0))],
        out_specs=pl.BlockSpec((c, nh * dv), lambda i: (i, 0)),
        out_shape=jax.ShapeDtypeStruct((lp, nh * dv), BF16),
        scratch_shapes=[pltpu.VMEM((nh, dv, dk), F32)],
        compiler_params=_params(("arbitrary",)),
        name="gla",
    )(proj, proj, proj, proj, small, wg_pad, b_g, norm_g)


def _causal_conv3(ext, w_ref):
    m1 = pltpu.roll(ext, 1, 0)
    m2 = pltpu.roll(ext, 2, 0)
    return (ext[HALO:] * w_ref[2:3, :] + m1[HALO:] * w_ref[1:2, :] + m2[HALO:] * w_ref[0:1, :])


def _merge_kernel(oa_ref, scb_ref, scc_ref, sch_ref, scc_h_ref, sch_h_ref, oc_ref,
                  ga_ref, gb_ref, gc_ref, gbias_ref, cw_ref, wa_ref, wb_ref, wc_ref, o_ref):
    keep = (pl.program_id(0) > 0).astype(F32)
    p_halo = scc_h_ref[...].astype(F32) * sch_h_ref[...].astype(F32) * keep
    p = scc_ref[...].astype(F32) * sch_ref[...].astype(F32)
    conv = _causal_conv3(jnp.concatenate([p_halo, p], axis=0), cw_ref)
    ub = (scb_ref[...].astype(F32) * conv).astype(BF16)
    ya = _dot(oa_ref[...], wa_ref[...])
    yb = _dot(ub, wb_ref[...])
    yc = _dot(oc_ref[...], wc_ref[...])
    mix = (_sigmoid(ga_ref[...].astype(F32) + gbias_ref[0:1, :]) * ya
           + _sigmoid(gb_ref[...].astype(F32) + gbias_ref[1:2, :]) * yb
           + _sigmoid(gc_ref[...].astype(F32) + gbias_ref[2:3, :]) * yc)
    o_ref[...] = mix.astype(o_ref.dtype)


def _merge(proj, oa, oc, gate_b, conv_w, w_a_o, w_b_o, w_c_o, layer, tm):
    lp = proj.shape[0]
    d = D_MODEL
    ch = CONV_CH
    hb = tm // HALO

    def col(c0, width):
        return lambda i: (i, c0 // width)

    def halo(c0):
        return lambda i: (jnp.maximum(i * hb - 1, 0), c0 // ch)

    def const(shape):
        return pl.BlockSpec(shape, lambda i: (0, 0), pipeline_mode=pl.Buffered(1))

    def weight(k):
        return pl.BlockSpec((None, k, d), lambda i: (layer, 0, 0), pipeline_mode=pl.Buffered(1))

    return pl.pallas_call(
        _merge_kernel,
        grid=(lp // tm,),
        in_specs=[pl.BlockSpec((tm, FOX_WIDTH), lambda i: (i, 0)),
                  pl.BlockSpec((tm, ch), col(COL_SCB, ch)),
                  pl.BlockSpec((tm, ch), col(COL_SCC, ch)),
                  pl.BlockSpec((tm, ch), col(COL_SCH, ch)),
                  pl.BlockSpec((HALO, ch), halo(COL_SCC)),
                  pl.BlockSpec((HALO, ch), halo(COL_SCH)),
                  pl.BlockSpec((tm, GLA_HEADS * GLA_DV), lambda i: (i, 0)),
                  pl.BlockSpec((tm, d), col(COL_GA, d)),
                  pl.BlockSpec((tm, d), col(COL_GB, d)),
                  pl.BlockSpec((tm, d), col(COL_GC, d)),
                  const((3, d)),
                  const((3, ch)),
                  weight(FOX_WIDTH),
                  weight(ch),
                  weight(GLA_HEADS * GLA_DV)],
        out_specs=pl.BlockSpec((tm, d), lambda i: (i, 0)),
        out_shape=jax.ShapeDtypeStruct((lp, d), BF16),
        compiler_params=_params(("parallel",)),
        name="merge",
    )(oa, proj, proj, proj, proj, proj, oc, proj, proj, proj,
      gate_b.reshape(3, d), conv_w, w_a_o, w_b_o, w_c_o)


def _resid_matmul_kernel(x_ref, w_ref, h_ref, g_ref, ho_ref, no_ref, acc_ref):
    kk = pl.program_id(1)

    @pl.when(kk == 0)
    def _():
        acc_ref[...] = jnp.zeros(acc_ref.shape, F32)

    acc_ref[...] += _dot(x_ref[...], w_ref[...])

    @pl.when(kk == pl.num_programs(1) - 1)
    def _():
        hn = h_ref[...] + acc_ref[...]
        ho_ref[...] = hn
        y = hn * lax.rsqrt(jnp.mean(hn * hn, axis=-1, keepdims=True) + EPS)
        no_ref[...] = (y * g_ref[...]).astype(no_ref.dtype)


def _resid_matmul(x, w, layer, h, g, norm_dtype, tm, tk, name):
    lp, k = x.shape
    d = w.shape[2]
    return pl.pallas_call(
        _resid_matmul_kernel,
        grid=(lp // tm, k // tk),
        in_specs=[pl.BlockSpec((tm, tk), lambda i, kk: (i, kk)),
                  pl.BlockSpec((None, tk, d), lambda i, kk: (layer, kk, 0)),
                  pl.BlockSpec((tm, d), lambda i, kk: (i, 0)),
                  pl.BlockSpec((1, d), lambda i, kk: (0, 0))],
        out_specs=[pl.BlockSpec((tm, d), lambda i, kk: (i, 0)),
                   pl.BlockSpec((tm, d), lambda i, kk: (i, 0))],
        out_shape=[jax.ShapeDtypeStruct((lp, d), F32),
                   jax.ShapeDtypeStruct((lp, d), norm_dtype)],
        scratch_shapes=[pltpu.VMEM((tm, d), F32)],
        compiler_params=_params(("parallel", "arbitrary")),
        name=name,
    )(x, w, h, g.reshape(1, d))


def _matmul_resid_kernel(x_ref, w_ref, h_ref, o_ref):
    o_ref[...] = h_ref[...] + _dot(x_ref[...], w_ref[...])


def _matmul_resid(x, w, layer, h, tm, tn, name):
    lp, k = x.shape
    n = w.shape[2]
    return pl.pallas_call(
        _matmul_resid_kernel,
        grid=(lp // tm, n // tn),
        in_specs=[pl.BlockSpec((tm, k), lambda i, j: (i, 0)),
                  pl.BlockSpec((None, k, tn), lambda i, j: (layer, 0, j)),
                  pl.BlockSpec((tm, tn), lambda i, j: (i, j))],
        out_specs=pl.BlockSpec((tm, tn), lambda i, j: (i, j)),
        out_shape=jax.ShapeDtypeStruct((lp, n), F32),
        compiler_params=_params(("parallel", "arbitrary")),
        name=name,
    )(x, w, h)


def _rmsnorm_kernel(h_ref, g_ref, o_ref):
    x = h_ref[...]
    y = x * lax.rsqrt(jnp.mean(x * x, axis=-1, keepdims=True) + EPS)
    o_ref[...] = (y * g_ref[...]).astype(o_ref.dtype)


def _rmsnorm(h, g, tm):
    lp, d = h.shape
    return pl.pallas_call(
        _rmsnorm_kernel,
        grid=(lp // tm,),
        in_specs=[pl.BlockSpec((tm, d), lambda i: (i, 0)),
                  pl.BlockSpec((1, d), lambda i: (0, 0))],
        out_specs=pl.BlockSpec((tm, d), lambda i: (i, 0)),
        out_shape=jax.ShapeDtypeStruct((lp, d), BF16),
        compiler_params=_params(("parallel",)),
        name="rmsnorm",
    )(h, g.reshape(1, d))


def _final_norm_kernel(h_ref, hn_ref, g_ref, o_ref):
    x = jnp.concatenate([h_ref[N_META:, :], hn_ref[...]], axis=0)
    y = x * lax.rsqrt(jnp.mean(x * x, axis=-1, keepdims=True) + EPS)
    o_ref[...] = y * g_ref[...]


def _final_norm(h, g, seq, tm):
    lp, d = h.shape
    mb = tm // N_META
    last = lp // N_META - 1
    return pl.pallas_call(
        _final_norm_kernel,
        grid=(-(-seq // tm),),
        in_specs=[pl.BlockSpec((tm, d), lambda i: (i, 0)),
                  pl.BlockSpec((N_META, d), lambda i: (jnp.minimum((i + 1) * mb, last), 0)),
                  pl.BlockSpec((1, d), lambda i: (0, 0))],
        out_specs=pl.BlockSpec((tm, d), lambda i: (i, 0)),
        out_shape=jax.ShapeDtypeStruct((seq, d), F32),
        compiler_params=_params(("parallel",)),
        name="final_norm",
    )(h, h, g.reshape(1, d))


def _mlp_up_kernel(x_ref, xh_ref, wg_ref, wu_ref, cg_ref, cu_ref, o_ref, *, n_sub):
    rs = x_ref.shape[0] // n_sub
    keep = (pl.program_id(0) > 0).astype(BF16)
    wg = wg_ref[...].astype(BF16)
    wu = wu_ref[...].astype(BF16)
    for s in range(n_sub):
        if s == 0:
            xe = jnp.concatenate([xh_ref[...] * keep, x_ref[:rs, :]], axis=0)
        else:
            xe = x_ref[s * rs - HALO:(s + 1) * rs, :]
        ug = _causal_conv3(_dot(xe, wg), cg_ref)
        uu = _causal_conv3(_dot(xe, wu), cu_ref)
        o_ref[s * rs:(s + 1) * rs, :] = (ug * _sigmoid(ug) * uu).astype(o_ref.dtype)


def _mlp_up(xn, w_up, layer, conv_w, tm, tn):
    lp, d = xn.shape
    nj = D_FF // tn
    hb = tm // HALO
    n_sub = 4 if tm % (4 * HALO) == 0 else 1
    return pl.pallas_call(
        functools.partial(_mlp_up_kernel, n_sub=n_sub),
        grid=(lp // tm, nj),
        in_specs=[pl.BlockSpec((tm, d), lambda i, j: (i, 0)),
                  pl.BlockSpec((HALO, d), lambda i, j: (jnp.maximum(i * hb - 1, 0), 0)),
                  pl.BlockSpec((None, d, tn), lambda i, j: (layer, 0, j)),
                  pl.BlockSpec((None, d, tn), lambda i, j: (layer, 0, nj + j)),
                  pl.BlockSpec((3, tn), lambda i, j: (0, j)),
                  pl.BlockSpec((3, tn), lambda i, j: (0, nj + j))],
        out_specs=pl.BlockSpec((tm, tn), lambda i, j: (i, j)),
        out_shape=jax.ShapeDtypeStruct((lp, D_FF), BF16),
        compiler_params=_params(("parallel", "arbitrary")),
        name="mlp_up",
    )(xn, xn, w_up, w_up, conv_w, conv_w)


SRC_FA = 3 * FOX_WIDTH
SRC_MID = SRC_FA + FOX_HEADS
SRC_GLR = SRC_MID + 3 * CONV_CH + 2 * GLA_HEADS * GLA_DK + 2 * GLA_HEADS * GLA_DV
SRC_GATE = SRC_GLR + GLA_RANK
def _in_proj_kernel(x_ref, w_ref, s_ref, o_ref):
    w = w_ref[...].astype(BF16)
    o_ref[...] = (_dot_nt(x_ref[...], w) * s_ref[...]).astype(o_ref.dtype)


def _in_proj(xn, w_t, layer, colscale, tm, tn):
    lp, k = xn.shape
    assert COL_QA % tn == 0 and COL_SCB % tn == 0 and N_MAIN % tn == 0

    assert SRC_GATE % 8 == 0 and SRC_MID % 8 == 0 and tn % 8 == 0

    def src_row(j):
        c8 = j * (tn // 8)
        r8 = jnp.where(c8 < COL_QA // 8, SRC_GATE // 8 + c8,
                       jnp.where(c8 < COL_SCB // 8, c8 - COL_QA // 8,
                                 (SRC_MID - COL_SCB) // 8 + c8))
        return r8 * 8

    return pl.pallas_call(
        _in_proj_kernel,
        grid=(lp // tm, N_MAIN // tn),
        in_specs=[pl.BlockSpec((tm, k), lambda i, j: (i, 0)),
                  pl.BlockSpec((None, pl.Element(tn), pl.Element(k)),
                               lambda i, j: (layer, src_row(j), 0)),
                  pl.BlockSpec((1, tn), lambda i, j: (0, j))],
        out_specs=pl.BlockSpec((tm, tn), lambda i, j: (i, j)),
        out_shape=jax.ShapeDtypeStruct((lp, N_MAIN), BF16),
        compiler_params=_params(("parallel", "arbitrary")),
        name="in_proj",
    )(xn, w_t, colscale)


def _in_proj_small_kernel(x_ref, w_ref, o_ref):
    o_ref[...] = _dot_nt(x_ref[...], w_ref[...].astype(BF16))


def _in_proj_small(xn, w_small_t, layer, tm):
    lp, k = xn.shape
    return pl.pallas_call(
        _in_proj_small_kernel,
        grid=(lp // tm,),
        in_specs=[pl.BlockSpec((tm, k), lambda i: (i, 0)),
                  pl.BlockSpec((None, N_SMALL, k), lambda i: (layer, 0, 0))],
        out_specs=pl.BlockSpec((tm, N_SMALL), lambda i: (i, 0)),
        out_shape=jax.ShapeDtypeStruct((lp, N_SMALL), F32),
        compiler_params=_params(("parallel",)),
        name="in_proj_small",
    )(xn, w_small_t)


def _small_w_in(w_t):
    pad = jnp.zeros((w_t.shape[0], N_SMALL - FOX_HEADS - GLA_RANK, w_t.shape[2]), w_t.dtype)
    return jnp.concatenate([w_t[:, SRC_FA:SRC_MID], w_t[:, SRC_GLR:SRC_GATE], pad], axis=1)


def _main_colscale():
    s = jnp.ones((1, N_MAIN), F32)
    s = s.at[:, COL_QA:COL_QA + FOX_WIDTH].set(FOX_DH ** -0.5 * LOG2_E)
    s = s.at[:, COL_QC:COL_QC + GLA_HEADS * GLA_DK].set(GLA_DK ** -0.5)
    return s


def kernel(x, meta_tokens, norm1_g, w_in, fox_b_f, gate_b, conv_w, gla_w_g2, gla_b_g, gla_norm_g,
           w_a_o, w_b_o, w_c_o, w_o, norm2_g, w_up, mlp_conv_w, w_down, final_norm_g):
    assert x.shape[0] == 1 and x.shape[2] == D_MODEL
    depth = w_in.shape[0]
    seq = x.shape[1]
    l_real = N_META + seq
    lp = -(-l_real // ROW_TILE) * ROW_TILE
    tm = ROW_TILE
    blk = ROW_TILE

    h, xn = _embed(x[0], meta_tokens.astype(x.dtype), norm1_g[0], lp, tm)
    colscale = _main_colscale()
    w_t = jnp.swapaxes(w_in, 1, 2)
    w_small_t = _small_w_in(w_t)
    w_a_o, w_b_o, w_c_o, w_o, w_down = (w.astype(BF16) for w in (w_a_o, w_b_o, w_c_o, w_o, w_down))

    for l in range(depth):
        proj = _in_proj(xn, w_t, l, colscale, _row_tile(lp, 1664), 1024)
        small = _in_proj_small(xn, w_small_t, l, tm)

        f_blocks = small[:, :FOX_HEADS].T.reshape(FOX_HEADS, lp // 128, 128).transpose(1, 0, 2)
        c = _fox_cumsum(f_blocks, fox_b_f[l])
        c_blocks = c.transpose(1, 0, 2).reshape(FOX_HEADS, lp // blk, 1, blk)
        oa = _fox_attention(proj, c_blocks, blk, 320, 64)

        wg = gla_w_g2[l].reshape(GLA_RANK, GLA_HEADS, GLA_DK).transpose(1, 0, 2)
        wg_pad = jnp.zeros((GLA_HEADS, N_SMALL, GLA_DK), F32)
        wg_pad = wg_pad.at[:, SMALL_GLR:SMALL_GLR + GLA_RANK, :].set(wg)
        oc = _gla(proj, small, wg_pad, gla_b_g[l].reshape(GLA_HEADS, 1, GLA_DK),
                  gla_norm_g[l].reshape(GLA_HEADS, 1, GLA_DV))

        mix = _merge(proj, oa, oc, gate_b[l], conv_w[l], w_a_o, w_b_o, w_c_o, l, 320)
        h, xn2 = _resid_matmul(mix, w_o, l, h, norm2_g[l], BF16, 320, D_MODEL, "out_proj")

        act = _mlp_up(xn2, w_up, l, mlp_conv_w[l], _row_tile(lp, 1664), 512)
        h = _matmul_resid(act, w_down, l, h, _row_tile(lp, 832, 64), 512, "mlp_down")
        if l + 1 < depth:
            xn = _rmsnorm(h, norm1_g[l + 1], tm)
    return _final_norm(h, final_norm_g, seq, tm)[None]
```

```python
import functools

import jax
import jax.numpy as jnp
from jax import lax
from jax.experimental import pallas as pl
from jax.experimental.pallas import tpu as pltpu

F32 = jnp.float32
BF16 = jnp.bfloat16

D_MODEL = 2048
N_META = 16
EPS = 1e-6
NEG = -1e30
LOG2_E = 1.4426950408889634

FOX_HEADS = 8
FOX_DH = D_MODEL // 16
FOX_WIDTH = FOX_HEADS * FOX_DH
CONV_CH = D_MODEL // 2
GLA_HEADS = 4
GLA_DK = D_MODEL // 16
GLA_DV = D_MODEL // 8
GLA_RANK = 16
GLA_TAU = 16.0
GLA_CHUNK = 128
D_FF = D_MODEL * 11 // 4

COL_GA = 0
COL_GB = COL_GA + D_MODEL
COL_GC = COL_GB + D_MODEL
COL_QA = COL_GC + D_MODEL
COL_KA = COL_QA + FOX_WIDTH
COL_VA = COL_KA + FOX_WIDTH
COL_SCB = COL_VA + FOX_WIDTH
COL_SCC = COL_SCB + CONV_CH
COL_SCH = COL_SCC + CONV_CH
COL_QC = COL_SCH + CONV_CH
COL_KC = COL_QC + GLA_HEADS * GLA_DK
COL_VC = COL_KC + GLA_HEADS * GLA_DK
COL_RC = COL_VC + GLA_HEADS * GLA_DV
N_MAIN = COL_RC + GLA_HEADS * GLA_DV
N_SMALL = 128
SMALL_GLR = FOX_HEADS

ROW_TILE = 640
HALO = 16
VMEM_LIMIT = 56 * 1024 * 1024


def _row_tile(lp, cap, granule=128):
    return max(t for t in range(granule, min(cap, lp) + 1, granule) if lp % t == 0)


def _params(sem, vmem=VMEM_LIMIT):
    return pltpu.CompilerParams(dimension_semantics=sem, vmem_limit_bytes=vmem)


def _sigmoid(x):
    return 1.0 / (1.0 + jnp.exp(-x))


def _log_sigmoid(x):
    return jnp.minimum(x, 0.0) - jnp.log(1.0 + jnp.exp(-jnp.abs(x)))


def _dot(a, b):
    return jnp.dot(a, b, preferred_element_type=F32)


def _dot_nt(a, b):
    return lax.dot_general(a, b, (((1,), (1,)), ((), ())), preferred_element_type=F32)


def _split3(x):
    hi = x.astype(BF16)
    r1 = x - hi.astype(F32)
    mid = r1.astype(BF16)
    lo = (r1 - mid.astype(F32)).astype(BF16)
    return hi, mid, lo


def _embed_kernel(x_ref, xp_ref, meta_ref, g_ref, h_ref, n_ref, *, l_real):
    i = pl.program_id(0)
    tm = h_ref.shape[0]
    top = jnp.where(i == 0, meta_ref[...], xp_ref[...])
    hb = jnp.concatenate([top, x_ref[:tm - N_META, :]], axis=0)
    row = i * tm + lax.broadcasted_iota(jnp.int32, (tm, 1), 0)
    hb = jnp.where(row < l_real, hb, 0.0)
    h_ref[...] = hb
    y = hb * lax.rsqrt(jnp.mean(hb * hb, axis=-1, keepdims=True) + EPS)
    n_ref[...] = (y * g_ref[...]).astype(n_ref.dtype)


def _embed(x2, meta, g, lp, tm):
    seq, d = x2.shape
    last_x_block = (seq - 1) // tm
    mb = tm // N_META
    return pl.pallas_call(
        functools.partial(_embed_kernel, l_real=N_META + seq),
        grid=(lp // tm,),
        in_specs=[pl.BlockSpec((tm, d), lambda i: (jnp.minimum(i, last_x_block), 0)),
                  pl.BlockSpec((N_META, d), lambda i: (jnp.maximum(i * mb - 1, 0), 0)),
                  pl.BlockSpec((N_META, d), lambda i: (0, 0)),
                  pl.BlockSpec((1, d), lambda i: (0, 0))],
        out_specs=[pl.BlockSpec((tm, d), lambda i: (i, 0)),
                   pl.BlockSpec((tm, d), lambda i: (i, 0))],
        out_shape=[jax.ShapeDtypeStruct((lp, d), F32),
                   jax.ShapeDtypeStruct((lp, d), BF16)],
        compiler_params=_params(("parallel",)),
        name="embed",
    )(x2, x2, meta, g.reshape(1, d))


def _fox_cumsum_kernel(f_ref, b_ref, c_ref):
    nblk = f_ref.shape[0]
    row = lax.broadcasted_iota(jnp.int32, (128, 128), 0)
    col = lax.broadcasted_iota(jnp.int32, (128, 128), 1)
    triu = jnp.where(row <= col, 1.0, 0.0).astype(BF16)

    def body(j, carry):
        logf = _log_sigmoid(f_ref[j] + b_ref[...]) * LOG2_E
        hi, mid, lo = _split3(logf)
        cs = _dot(hi, triu) + _dot(mid, triu) + _dot(lo, triu) + carry
        c_ref[j] = cs
        return cs[:, 127:128]

    lax.fori_loop(0, nblk, body, jnp.zeros((f_ref.shape[1], 1), F32))


def _fox_cumsum(f_blocks, b_f):
    nblk, nh, _ = f_blocks.shape
    return pl.pallas_call(
        _fox_cumsum_kernel,
        out_shape=jax.ShapeDtypeStruct((nblk, nh, 128), F32),
        name="fox_cumsum",
    )(f_blocks, b_f.reshape(nh, 1))


def _lane_tile(x, n):
    return jnp.concatenate([x] * n, axis=1)


def _fox_kernel(q_ref, k_ref, v_ref, c_ref, o_ref, vx_sc, s_sc, p_sc, a_sc, m_sc, acc_sc,
                *, blk, part, sub, hg):
    i = pl.program_id(1)
    dh = q_ref.shape[1] // hg
    heads = range(hg)

    @pl.when(i == 0)
    def _():
        for h in heads:
            vx_sc[h, :, :dh] = v_ref[:, h * dh:(h + 1) * dh]
            vx_sc[h, :, dh:] = jnp.ones((vx_sc.shape[1], dh), BF16)

    m_sc[...] = jnp.full(m_sc.shape, NEG, F32)
    acc_sc[...] = jnp.zeros(acc_sc.shape, F32)

    def rows_of(j):
        return pl.ds(pl.multiple_of(j * blk, blk), blk)

    def logits(j, slot):
        for h in heads:
            kj = k_ref[rows_of(j), h * dh:(h + 1) * dh]
            for a in range(0, blk, part):
                s_sc[slot, h, a:a + part, :] = _dot_nt(q_ref[a:a + part, h * dh:(h + 1) * dh], kj)

    def softmax(j, slot, masked):
        def biased(h, r):
            s = s_sc[slot, h, r:r + sub, :] - c_ref[h, j]
            if masked:
                qpos = r + lax.broadcasted_iota(jnp.int32, (sub, blk), 0)
                kpos = lax.broadcasted_iota(jnp.int32, (sub, blk), 1)
                s = jnp.where(kpos <= qpos, s, NEG)
            return s

        for h in heads:
            for r in range(0, blk, sub):
                m_prev = m_sc[h, r:r + sub, :]
                m_new = jnp.maximum(m_prev, jnp.max(biased(h, r), axis=-1, keepdims=True))
                a_sc[slot, h, r:r + sub, :] = jnp.exp2(m_prev - m_new)
                m_sc[h, r:r + sub, :] = m_new
        for h in heads:
            for r in range(0, blk, sub):
                m_new = _lane_tile(m_sc[h, r:r + sub, :], blk // 128)
                p_sc[slot, h, r:r + sub, :] = jnp.exp2(biased(h, r) - m_new).astype(BF16)

    def values(j, slot):
        for h in heads:
            vj = vx_sc[h, rows_of(j), :]
            for a in range(0, blk, part):
                acc_sc[h, a:a + part, :] = (
                    acc_sc[h, a:a + part, :] * _lane_tile(a_sc[slot, h, a:a + part, :], 2)
                    + _dot(p_sc[slot, h, a:a + part, :], vj))

    def step(j, slot, masked, prefetch, prev):
        if prefetch:
            logits(j + 1, 1 - slot)
        if prev:
            values(j - 1, 1 - slot)
        softmax(j, slot, masked)

    def step_pair(t, _):
        step(2 * t + 1, 1, masked=False, prefetch=True, prev=True)
        step(2 * t + 2, 0, masked=False, prefetch=True, prev=True)
        return 0

    logits(0, 0)

    @pl.when(i == 0)
    def _():
        softmax(0, 0, masked=True)
        values(0, 0)

    @pl.when(i > 0)
    def _():
        step(0, 0, masked=False, prefetch=True, prev=False)
        lax.fori_loop(0, (i - 1) // 2, step_pair, 0)

        @pl.when(i % 2 == 0)
        def _():
            step(i - 1, 1, masked=False, prefetch=True, prev=True)
            step(i, 0, masked=True, prefetch=False, prev=True)
            values(i, 0)

        @pl.when(i % 2 == 1)
        def _():
            step(i, 1, masked=True, prefetch=False, prev=True)
            values(i, 1)

    for h in heads:
        o_ref[:, h * dh:(h + 1) * dh] = (acc_sc[h, :, :dh] / acc_sc[h, :, dh:]).astype(o_ref.dtype)


def _fox_attention(proj, c_blocks, blk, part, sub, hg):
    lp = proj.shape[0]
    nb = lp // blk
    dh = FOX_DH
    w = hg * dh
    kern = functools.partial(_fox_kernel, blk=blk, part=part, sub=sub, hg=hg)
    return pl.pallas_call(
        kern,
        grid=(FOX_HEADS // hg, nb),
        in_specs=[pl.BlockSpec((blk, w), lambda g, i: (i, COL_QA // w + g)),
                  pl.BlockSpec((lp, w), lambda g, i: (0, COL_KA // w + g)),
                  pl.BlockSpec((lp, w), lambda g, i: (0, COL_VA // w + g)),
                  pl.BlockSpec((hg, nb, 1, blk), lambda g, i: (g, 0, 0, 0))],
        out_specs=pl.BlockSpec((blk, w), lambda g, i: (i, g)),
        out_shape=jax.ShapeDtypeStruct((lp, FOX_WIDTH), BF16),
        scratch_shapes=[pltpu.VMEM((hg, lp, 2 * dh), BF16),
                        pltpu.VMEM((2, hg, blk, blk), F32),
                        pltpu.VMEM((2, hg, blk, blk), BF16),
                        pltpu.VMEM((2, hg, blk, 128), F32),
                        pltpu.VMEM((hg, blk, 128), F32),
                        pltpu.VMEM((hg, blk, 2 * dh), F32)],
        compiler_params=_params(("arbitrary", "arbitrary")),
        name="fox_attention",
    )(proj, proj, proj, c_blocks)


def _segment_reference_rows(b, n, sub8):
    c = b.shape[0]
    half = n // 2
    if n >= 16:
        pieces = [jnp.broadcast_to(b[m * n + half - 1:m * n + half, :], (n, b.shape[1]))
                  for m in range(c // n)]
        return pieces[0] if len(pieces) == 1 else jnp.concatenate(pieces, axis=0)
    b3 = b.reshape(c // 8, 8, b.shape[1])

    def bc(r):
        return jnp.broadcast_to(b3[:, r:r + 1, :], b3.shape).reshape(b.shape)

    out = bc(8 - n + half - 1)
    for start in range(8 - 2 * n, -1, -n):
        out = jnp.where(sub8 < start + n, bc(start + half - 1), out)
    return out


def _gla_kernel(q_ref, k_ref, v_ref, r_ref, sm_ref, wg_ref, bg_ref, ng_ref, o_ref, st_ref):
    c, dk, dv = GLA_CHUNK, GLA_DK, GLA_DV

    @pl.when(pl.program_id(0) == 0)
    def _():
        st_ref[...] = jnp.zeros(st_ref.shape, F32)

    t_idx = lax.broadcasted_iota(jnp.int32, (c, c), 0)
    s_idx = lax.broadcasted_iota(jnp.int32, (c, c), 1)
    tril = jnp.where(s_idx <= t_idx, 1.0, 0.0).astype(BF16)
    sub8 = t_idx & 7
    sm = sm_ref[...].astype(BF16)

    for h in range(GLA_HEADS):
        q = q_ref[:, h * dk:(h + 1) * dk].astype(F32)
        k = k_ref[:, h * dk:(h + 1) * dk].astype(F32)
        v = v_ref[:, h * dv:(h + 1) * dv]
        z = _dot(sm, wg_ref[h].astype(BF16)) + bg_ref[h]
        g = _log_sigmoid(z) * (1.0 / GLA_TAU)
        hi, mid, lo = _split3(g)
        b = _dot(tril, hi) + _dot(tril, mid) + _dot(tril, lo)

        att = jnp.where(t_idx == s_idx, _dot_nt(q.astype(BF16), k.astype(BF16)), 0.0)
        n = c
        while n >= 2:
            upper = (t_idx & (n - 1)) >= n // 2
            ref = _segment_reference_rows(b, n, sub8)
            qs = jnp.where(upper, q * jnp.exp(jnp.minimum(b - ref, 0.0)), 0.0).astype(BF16)
            ks = jnp.where(upper, 0.0, k * jnp.exp(jnp.minimum(ref - b, 0.0))).astype(BF16)
            a = _dot_nt(qs, ks)
            if n < c:
                shift = n.bit_length() - 1
                a = jnp.where((t_idx >> shift) == (s_idx >> shift), a, 0.0)
            att = att + a
            n //= 2

        st = st_ref[h]
        o = _dot(att.astype(BF16), v) + _dot_nt((q * jnp.exp(b)).astype(BF16), st.astype(BF16))
        b_last = b[c - 1:c, :]
        ke = (k * jnp.exp(b_last - b)).astype(BF16)
        st_ref[h] = st * jnp.exp(b_last) + lax.dot_general(
            v, ke, (((0,), (0,)), ((), ())), preferred_element_type=F32)

        on = o * lax.rsqrt(jnp.mean(o * o, axis=-1, keepdims=True) + EPS) * ng_ref[h]
        r = r_ref[:, h * dv:(h + 1) * dv].astype(F32)
        o_ref[:, h * dv:(h + 1) * dv] = (r * _sigmoid(r) * on).astype(o_ref.dtype)


def _gla(proj, small, wg_pad, b_g, norm_g):
    lp = proj.shape[0]
    c, dk, dv, nh = GLA_CHUNK, GLA_DK, GLA_DV, GLA_HEADS
    assert dk == c
    return pl.pallas_call(
        _gla_kernel,
        grid=(lp // c,),
        in_specs=[pl.BlockSpec((c, nh * dk), lambda i: (i, COL_QC // (nh * dk))),
                  pl.BlockSpec((c, nh * dk), lambda i: (i, COL_KC // (nh * dk))),
                  pl.BlockSpec((c, nh * dv), lambda i: (i, COL_VC // (nh * dv))),
                  pl.BlockSpec((c, nh * dv), lambda i: (i, COL_RC // (nh * dv))),
                  pl.BlockSpec((c, N_SMALL), lambda i: (i, 0)),
                  pl.BlockSpec((nh, N_SMALL, dk), lambda i: (0, 0, 0)),
                  pl.BlockSpec((nh, 1, dk), lambda i: (0, 0, 0)),
                  pl.BlockSpec((nh, 1, dv), lambda i: (0, 0, 0))],
        out_specs=pl.BlockSpec((c, nh * dv), lambda i: (i, 0)),
        out_shape=jax.ShapeDtypeStruct((lp, nh * dv), BF16),
        scratch_shapes=[pltpu.VMEM((nh, dv, dk), F32)],
        compiler_params=_params(("arbitrary",)),
        name="gla",
    )(proj, proj, proj, proj, small, wg_pad, b_g, norm_g)


def _causal_conv3(ext, w_ref):
    m1 = pltpu.roll(ext, 1, 0)
    m2 = pltpu.roll(ext, 2, 0)
    return (ext[HALO:] * w_ref[2:3, :] + m1[HALO:] * w_ref[1:2, :] + m2[HALO:] * w_ref[0:1, :])


def _merge_kernel(oa_ref, scb_ref, scc_ref, sch_ref, scc_h_ref, sch_h_ref, oc_ref,
                  ga_ref, gb_ref, gc_ref, gbias_ref, cw_ref, wa_ref, wb_ref, wc_ref, o_ref):
    keep = (pl.program_id(0) > 0).astype(F32)
    p_halo = scc_h_ref[...].astype(F32) * sch_h_ref[...].astype(F32) * keep
    p = scc_ref[...].astype(F32) * sch_ref[...].astype(F32)
    conv = _causal_conv3(jnp.concatenate([p_halo, p], axis=0), cw_ref)
    ub = (scb_ref[...].astype(F32) * conv).astype(BF16)
    ya = _dot(oa_ref[...], wa_ref[...])
    yb = _dot(ub, wb_ref[...])
    yc = _dot(oc_ref[...], wc_ref[...])
    mix = (_sigmoid(ga_ref[...].astype(F32) + gbias_ref[0:1, :]) * ya
           + _sigmoid(gb_ref[...].astype(F32) + gbias_ref[1:2, :]) * yb
           + _sigmoid(gc_ref[...].astype(F32) + gbias_ref[2:3, :]) * yc)
    o_ref[...] = mix.astype(o_ref.dtype)


def _merge(proj, oa, oc, gate_b, conv_w, w_a_o, w_b_o, w_c_o, layer, tm):
    lp = proj.shape[0]
    d = D_MODEL
    ch = CONV_CH
    hb = tm // HALO

    def col(c0, width):
        return lambda i: (i, c0 // width)

    def halo(c0):
        return lambda i: (jnp.maximum(i * hb - 1, 0), c0 // ch)

    def const(shape):
        return pl.BlockSpec(shape, lambda i: (0, 0), pipeline_mode=pl.Buffered(1))

    def weight(k):
        return pl.BlockSpec((None, k, d), lambda i: (layer, 0, 0), pipeline_mode=pl.Buffered(1))

    return pl.pallas_call(
        _merge_kernel,
        grid=(lp // tm,),
        in_specs=[pl.BlockSpec((tm, FOX_WIDTH), lambda i: (i, 0)),
                  pl.BlockSpec((tm, ch), col(COL_SCB, ch)),
                  pl.BlockSpec((tm, ch), col(COL_SCC, ch)),
                  pl.BlockSpec((tm, ch), col(COL_SCH, ch)),
                  pl.BlockSpec((HALO, ch), halo(COL_SCC)),
                  pl.BlockSpec((HALO, ch), halo(COL_SCH)),
                  pl.BlockSpec((tm, GLA_HEADS * GLA_DV), lambda i: (i, 0)),
                  pl.BlockSpec((tm, d), col(COL_GA, d)),
                  pl.BlockSpec((tm, d), col(COL_GB, d)),
                  pl.BlockSpec((tm, d), col(COL_GC, d)),
                  const((3, d)),
                  const((3, ch)),
                  weight(FOX_WIDTH),
                  weight(ch),
                  weight(GLA_HEADS * GLA_DV)],
        out_specs=pl.BlockSpec((tm, d), lambda i: (i, 0)),
        out_shape=jax.ShapeDtypeStruct((lp, d), BF16),
        compiler_params=_params(("parallel",)),
        name="merge",
    )(oa, proj, proj, proj, proj, proj, oc, proj, proj, proj,
      gate_b.reshape(3, d), conv_w, w_a_o, w_b_o, w_c_o)


def _resid_matmul_kernel(x_ref, w_ref, h_ref, g_ref, ho_ref, no_ref, acc_ref):
    kk = pl.program_id(1)

    @pl.when(kk == 0)
    def _():
        acc_ref[...] = jnp.zeros(acc_ref.shape, F32)

    acc_ref[...] += _dot(x_ref[...], w_ref[...])

    @pl.when(kk == pl.num_programs(1) - 1)
    def _():
        hn = h_ref[...] + acc_ref[...]
        ho_ref[...] = hn
        y = hn * lax.rsqrt(jnp.mean(hn * hn, axis=-1, keepdims=True) + EPS)
        no_ref[...] = (y * g_ref[...]).astype(no_ref.dtype)


def _resid_matmul(x, w, layer, h, g, norm_dtype, tm, tk, name):
    lp, k = x.shape
    d = w.shape[2]
    return pl.pallas_call(
        _resid_matmul_kernel,
        grid=(lp // tm, k // tk),
        in_specs=[pl.BlockSpec((tm, tk), lambda i, kk: (i, kk)),
                  pl.BlockSpec((None, tk, d), lambda i, kk: (layer, kk, 0)),
                  pl.BlockSpec((tm, d), lambda i, kk: (i, 0)),
                  pl.BlockSpec((1, d), lambda i, kk: (0, 0))],
        out_specs=[pl.BlockSpec((tm, d), lambda i, kk: (i, 0)),
                   pl.BlockSpec((tm, d), lambda i, kk: (i, 0))],
        out_shape=[jax.ShapeDtypeStruct((lp, d), F32),
                   jax.ShapeDtypeStruct((lp, d), norm_dtype)],
        scratch_shapes=[pltpu.VMEM((tm, d), F32)],
        compiler_params=_params(("parallel", "arbitrary")),
        name=name,
    )(x, w, h, g.reshape(1, d))


def _matmul_resid_kernel(x_ref, w_ref, h_ref, o_ref):
    o_ref[...] = h_ref[...] + _dot(x_ref[...], w_ref[...])


def _matmul_resid(x, w, layer, h, tm, tn, name):
    lp, k = x.shape
    n = w.shape[2]
    return pl.pallas_call(
        _matmul_resid_kernel,
        grid=(lp // tm, n // tn),
        in_specs=[pl.BlockSpec((tm, k), lambda i, j: (i, 0)),
                  pl.BlockSpec((None, k, tn), lambda i, j: (layer, 0, j)),
                  pl.BlockSpec((tm, tn), lambda i, j: (i, j))],
        out_specs=pl.BlockSpec((tm, tn), lambda i, j: (i, j)),
        out_shape=jax.ShapeDtypeStruct((lp, n), F32),
        compiler_params=_params(("parallel", "arbitrary")),
        name=name,
    )(x, w, h)


def _rmsnorm_kernel(h_ref, g_ref, o_ref):
    x = h_ref[...]
    y = x * lax.rsqrt(jnp.mean(x * x, axis=-1, keepdims=True) + EPS)
    o_ref[...] = (y * g_ref[...]).astype(o_ref.dtype)


def _rmsnorm(h, g, tm):
    lp, d = h.shape
    return pl.pallas_call(
        _rmsnorm_kernel,
        grid=(lp // tm,),
        in_specs=[pl.BlockSpec((tm, d), lambda i: (i, 0)),
                  pl.BlockSpec((1, d), lambda i: (0, 0))],
        out_specs=pl.BlockSpec((tm, d), lambda i: (i, 0)),
        out_shape=jax.ShapeDtypeStruct((lp, d), BF16),
        compiler_params=_params(("parallel",)),
        name="rmsnorm",
    )(h, g.reshape(1, d))


def _final_norm_kernel(h_ref, hn_ref, g_ref, o_ref):
    x = jnp.concatenate([h_ref[N_META:, :], hn_ref[...]], axis=0)
    y = x * lax.rsqrt(jnp.mean(x * x, axis=-1, keepdims=True) + EPS)
    o_ref[...] = y * g_ref[...]


def _final_norm(h, g, seq, tm):
    lp, d = h.shape
    mb = tm // N_META
    last = lp // N_META - 1
    return pl.pallas_call(
        _final_norm_kernel,
        grid=(-(-seq // tm),),
        in_specs=[pl.BlockSpec((tm, d), lambda i: (i, 0)),
                  pl.BlockSpec((N_META, d), lambda i: (jnp.minimum((i + 1) * mb, last), 0)),
                  pl.BlockSpec((1, d), lambda i: (0, 0))],
        out_specs=pl.BlockSpec((tm, d), lambda i: (i, 0)),
        out_shape=jax.ShapeDtypeStruct((seq, d), F32),
        compiler_params=_params(("parallel",)),
        name="final_norm",
    )(h, h, g.reshape(1, d))


def _mlp_up_kernel(x_ref, xh_ref, wg_ref, wu_ref, cg_ref, cu_ref, o_ref, *, n_sub):
    rs = x_ref.shape[0] // n_sub
    keep = (pl.program_id(0) > 0).astype(BF16)
    wg = wg_ref[...].astype(BF16)
    wu = wu_ref[...].astype(BF16)
    for s in range(n_sub):
        if s == 0:
            xe = jnp.concatenate([xh_ref[...] * keep, x_ref[:rs, :]], axis=0)
        else:
            xe = x_ref[s * rs - HALO:(s + 1) * rs, :]
        ug = _causal_conv3(_dot(xe, wg), cg_ref)
        uu = _causal_conv3(_dot(xe, wu), cu_ref)
        o_ref[s * rs:(s + 1) * rs, :] = (ug * _sigmoid(ug) * uu).astype(o_ref.dtype)


def _mlp_up(xn, w_up, layer, conv_w, tm, tn):
    lp, d = xn.shape
    nj = D_FF // tn
    hb = tm // HALO
    n_sub = 4 if tm % (4 * HALO) == 0 else 1
    return pl.pallas_call(
        functools.partial(_mlp_up_kernel, n_sub=n_sub),
        grid=(lp // tm, nj),
        in_specs=[pl.BlockSpec((tm, d), lambda i, j: (i, 0)),
                  pl.BlockSpec((HALO, d), lambda i, j: (jnp.maximum(i * hb - 1, 0), 0)),
                  pl.BlockSpec((None, d, tn), lambda i, j: (layer, 0, j)),
                  pl.BlockSpec((None, d, tn), lambda i, j: (layer, 0, nj + j)),
                  pl.BlockSpec((3, tn), lambda i, j: (0, j)),
                  pl.BlockSpec((3, tn), lambda i, j: (0, nj + j))],
        out_specs=pl.BlockSpec((tm, tn), lambda i, j: (i, j)),
        out_shape=jax.ShapeDtypeStruct((lp, D_FF), BF16),
        compiler_params=_params(("parallel", "arbitrary")),
        name="mlp_up",
    )(xn, xn, w_up, w_up, conv_w, conv_w)


SRC_FA = 3 * FOX_WIDTH
SRC_MID = SRC_FA + FOX_HEADS
SRC_GLR = SRC_MID + 3 * CONV_CH + 2 * GLA_HEADS * GLA_DK + 2 * GLA_HEADS * GLA_DV
SRC_GATE = SRC_GLR + GLA_RANK
def _in_proj_kernel(x_ref, w_ref, s_ref, o_ref):
    w = w_ref[...].astype(BF16)
    o_ref[...] = (_dot_nt(x_ref[...], w) * s_ref[...]).astype(o_ref.dtype)


def _in_proj(xn, w_t, layer, colscale, tm, tn):
    lp, k = xn.shape
    assert COL_QA % tn == 0 and COL_SCB % tn == 0 and N_MAIN % tn == 0

    assert SRC_GATE % 8 == 0 and SRC_MID % 8 == 0 and tn % 8 == 0

    def src_row(j):
        c8 = j * (tn // 8)
        r8 = jnp.where(c8 < COL_QA // 8, SRC_GATE // 8 + c8,
                       jnp.where(c8 < COL_SCB // 8, c8 - COL_QA // 8,
                                 (SRC_MID - COL_SCB) // 8 + c8))
        return r8 * 8

    return pl.pallas_call(
        _in_proj_kernel,
        grid=(lp // tm, N_MAIN // tn),
        in_specs=[pl.BlockSpec((tm, k), lambda i, j: (i, 0)),
                  pl.BlockSpec((None, pl.Element(tn), pl.Element(k)),
                               lambda i, j: (layer, src_row(j), 0)),
                  pl.BlockSpec((1, tn), lambda i, j: (0, j))],
        out_specs=pl.BlockSpec((tm, tn), lambda i, j: (i, j)),
        out_shape=jax.ShapeDtypeStruct((lp, N_MAIN), BF16),
        compiler_params=_params(("parallel", "arbitrary")),
        name="in_proj",
    )(xn, w_t, colscale)


def _in_proj_small_kernel(x_ref, w_ref, o_ref):
    o_ref[...] = _dot_nt(x_ref[...], w_ref[...].astype(BF16))


def _in_proj_small(xn, w_small_t, layer, tm):
    lp, k = xn.shape
    return pl.pallas_call(
        _in_proj_small_kernel,
        grid=(lp // tm,),
        in_specs=[pl.BlockSpec((tm, k), lambda i: (i, 0)),
                  pl.BlockSpec((None, N_SMALL, k), lambda i: (layer, 0, 0))],
        out_specs=pl.BlockSpec((tm, N_SMALL), lambda i: (i, 0)),
        out_shape=jax.ShapeDtypeStruct((lp, N_SMALL), F32),
        compiler_params=_params(("parallel",)),
        name="in_proj_small",
    )(xn, w_small_t)


def _small_w_in(w_t):
    pad = jnp.zeros((w_t.shape[0], N_SMALL - FOX_HEADS - GLA_RANK, w_t.shape[2]), w_t.dtype)
    return jnp.concatenate([w_t[:, SRC_FA:SRC_MID], w_t[:, SRC_GLR:SRC_GATE], pad], axis=1)


def _main_colscale():
    s = jnp.ones((1, N_MAIN), F32)
    s = s.at[:, COL_QA:COL_QA + FOX_WIDTH].set(FOX_DH ** -0.5 * LOG2_E)
    s = s.at[:, COL_QC:COL_QC + GLA_HEADS * GLA_DK].set(GLA_DK ** -0.5)
    return s


def kernel(x, meta_tokens, norm1_g, w_in, fox_b_f, gate_b, conv_w, gla_w_g2, gla_b_g, gla_norm_g,
           w_a_o, w_b_o, w_c_o, w_o, norm2_g, w_up, mlp_conv_w, w_down, final_norm_g):
    assert x.shape[0] == 1 and x.shape[2] == D_MODEL
    depth = w_in.shape[0]
    seq = x.shape[1]
    l_real = N_META + seq
    lp = -(-l_real // ROW_TILE) * ROW_TILE
    tm = ROW_TILE
    blk = ROW_TILE

    h, xn = _embed(x[0], meta_tokens.astype(x.dtype), norm1_g[0], lp, tm)
    colscale = _main_colscale()
    w_t = jnp.swapaxes(w_in, 1, 2)
    w_small_t = _small_w_in(w_t)
    w_a_o, w_b_o, w_c_o, w_o, w_down = (w.astype(BF16) for w in (w_a_o, w_b_o, w_c_o, w_o, w_down))

    for l in range(depth):
        proj = _in_proj(xn, w_t, l, colscale, _row_tile(lp, 1664), 1024)
        small = _in_proj_small(xn, w_small_t, l, tm)

        f_blocks = small[:, :FOX_HEADS].T.reshape(FOX_HEADS, lp // 128, 128).transpose(1, 0, 2)
        c = _fox_cumsum(f_blocks, fox_b_f[l])
        c_blocks = c.transpose(1, 0, 2).reshape(FOX_HEADS, lp // blk, 1, blk)
        oa = _fox_attention(proj, c_blocks, blk, 320, 64, 2)

        wg = gla_w_g2[l].reshape(GLA_RANK, GLA_HEADS, GLA_DK).transpose(1, 0, 2)
        wg_pad = jnp.zeros((GLA_HEADS, N_SMALL, GLA_DK), F32)
        wg_pad = wg_pad.at[:, SMALL_GLR:SMALL_GLR + GLA_RANK, :].set(wg)
        oc = _gla(proj, small, wg_pad, gla_b_g[l].reshape(GLA_HEADS, 1, GLA_DK),
                  gla_norm_g[l].reshape(GLA_HEADS, 1, GLA_DV))

        mix = _merge(proj, oa, oc, gate_b[l], conv_w[l], w_a_o, w_b_o, w_c_o, l, 320)
        h, xn2 = _resid_matmul(mix, w_o, l, h, norm2_g[l], BF16, 320, D_MODEL, "out_proj")

        act = _mlp_up(xn2, w_up, l, mlp_conv_w[l], _row_tile(lp, 1664), 512)
        h = _matmul_resid(act, w_down, l, h, _row_tile(lp, 832, 64), 512, "mlp_down")
        if l + 1 < depth:
            xn = _rmsnorm(h, norm1_g[l + 1], tm)
    return _final_norm(h, final_norm_g, seq, tm)[None]
```

```python
import functools

import jax
import jax.numpy as jnp
from jax import lax
from jax.experimental import pallas as pl
from jax.experimental.pallas import tpu as pltpu

F32 = jnp.float32
BF16 = jnp.bfloat16

D_MODEL = 2048
N_META = 16
EPS = 1e-6
NEG = -1e30
LOG2_E = 1.4426950408889634

FOX_HEADS = 8
FOX_DH = D_MODEL // 16
FOX_WIDTH = FOX_HEADS * FOX_DH
CONV_CH = D_MODEL // 2
GLA_HEADS = 4
GLA_DK = D_MODEL // 16
GLA_DV = D_MODEL // 8
GLA_RANK = 16
GLA_TAU = 16.0
GLA_CHUNK = 128
D_FF = D_MODEL * 11 // 4

COL_GA = 0
COL_GB = COL_GA + D_MODEL
COL_GC = COL_GB + D_MODEL
COL_QA = COL_GC + D_MODEL
COL_KA = COL_QA + FOX_WIDTH
COL_VA = COL_KA + FOX_WIDTH
COL_SCB = COL_VA + FOX_WIDTH
COL_SCC = COL_SCB + CONV_CH
COL_SCH = COL_SCC + CONV_CH
COL_QC = COL_SCH + CONV_CH
COL_KC = COL_QC + GLA_HEADS * GLA_DK
COL_VC = COL_KC + GLA_HEADS * GLA_DK
COL_RC = COL_VC + GLA_HEADS * GLA_DV
N_MAIN = COL_RC + GLA_HEADS * GLA_DV
N_SMALL = 128
SMALL_GLR = FOX_HEADS

ROW_TILE = 640
HALO = 16
VMEM_LIMIT = 56 * 1024 * 1024


def _row_tile(lp, cap, granule=128):
    return max(t for t in range(granule, min(cap, lp) + 1, granule) if lp % t == 0)


def _params(sem, vmem=VMEM_LIMIT):
    return pltpu.CompilerParams(dimension_semantics=sem, vmem_limit_bytes=vmem)


def _sigmoid(x):
    return 1.0 / (1.0 + jnp.exp(-x))


def _log_sigmoid(x):
    return jnp.minimum(x, 0.0) - jnp.log(1.0 + jnp.exp(-jnp.abs(x)))


def _dot(a, b):
    return jnp.dot(a, b, preferred_element_type=F32)


def _dot_nt(a, b):
    return lax.dot_general(a, b, (((1,), (1,)), ((), ())), preferred_element_type=F32)


def _split3(x):
    hi = x.astype(BF16)
    r1 = x - hi.astype(F32)
    mid = r1.astype(BF16)
    lo = (r1 - mid.astype(F32)).astype(BF16)
    return hi, mid, lo


def _embed_kernel(x_ref, xp_ref, meta_ref, g_ref, h_ref, n_ref, *, l_real):
    i = pl.program_id(0)
    tm = h_ref.shape[0]
    top = jnp.where(i == 0, meta_ref[...], xp_ref[...])
    hb = jnp.concatenate([top, x_ref[:tm - N_META, :]], axis=0)
    row = i * tm + lax.broadcasted_iota(jnp.int32, (tm, 1), 0)
    hb = jnp.where(row < l_real, hb, 0.0)
    h_ref[...] = hb
    y = hb * lax.rsqrt(jnp.mean(hb * hb, axis=-1, keepdims=True) + EPS)
    n_ref[...] = (y * g_ref[...]).astype(n_ref.dtype)


def _embed(x2, meta, g, lp, tm):
    seq, d = x2.shape
    last_x_block = (seq - 1) // tm
    mb = tm // N_META
    return pl.pallas_call(
        functools.partial(_embed_kernel, l_real=N_META + seq),
        grid=(lp // tm,),
        in_specs=[pl.BlockSpec((tm, d), lambda i: (jnp.minimum(i, last_x_block), 0)),
                  pl.BlockSpec((N_META, d), lambda i: (jnp.maximum(i * mb - 1, 0), 0)),
                  pl.BlockSpec((N_META, d), lambda i: (0, 0)),
                  pl.BlockSpec((1, d), lambda i: (0, 0))],
        out_specs=[pl.BlockSpec((tm, d), lambda i: (i, 0)),
                   pl.BlockSpec((tm, d), lambda i: (i, 0))],
        out_shape=[jax.ShapeDtypeStruct((lp, d), F32),
                   jax.ShapeDtypeStruct((lp, d), BF16)],
        compiler_params=_params(("parallel",)),
        name="embed",
    )(x2, x2, meta, g.reshape(1, d))


def _fox_cumsum_kernel(f_ref, b_ref, c_ref):
    nblk = f_ref.shape[0]
    row = lax.broadcasted_iota(jnp.int32, (128, 128), 0)
    col = lax.broadcasted_iota(jnp.int32, (128, 128), 1)
    triu = jnp.where(row <= col, 1.0, 0.0).astype(BF16)

    def body(j, carry):
        logf = _log_sigmoid(f_ref[j] + b_ref[...]) * LOG2_E
        hi, mid, lo = _split3(logf)
        cs = _dot(hi, triu) + _dot(mid, triu) + _dot(lo, triu) + carry
        c_ref[j] = cs
        return cs[:, 127:128]

    lax.fori_loop(0, nblk, body, jnp.zeros((f_ref.shape[1], 1), F32))


def _fox_cumsum(f_blocks, b_f):
    nblk, nh, _ = f_blocks.shape
    return pl.pallas_call(
        _fox_cumsum_kernel,
        out_shape=jax.ShapeDtypeStruct((nblk, nh, 128), F32),
        name="fox_cumsum",
    )(f_blocks, b_f.reshape(nh, 1))


def _lane_tile(x, n):
    return jnp.concatenate([x] * n, axis=1)


def _fox_kernel(q_ref, k_ref, v_ref, c_ref, o_ref, vx_sc, s_sc, p_sc, a_sc, m_sc, acc_sc,
                *, blk, part, sub, hg):
    i = pl.program_id(1)
    dh = q_ref.shape[1] // hg
    heads = range(hg)

    @pl.when(i == 0)
    def _():
        for h in heads:
            vx_sc[h, :, :dh] = v_ref[:, h * dh:(h + 1) * dh]
            vx_sc[h, :, dh:] = jnp.ones((vx_sc.shape[1], dh), BF16)

    m_sc[...] = jnp.full(m_sc.shape, NEG, F32)
    acc_sc[...] = jnp.zeros(acc_sc.shape, F32)

    def rows_of(j):
        return pl.ds(pl.multiple_of(j * blk, blk), blk)

    def logits(j, slot):
        for h in heads:
            kj = k_ref[rows_of(j), h * dh:(h + 1) * dh]
            for a in range(0, blk, part):
                s_sc[slot, h, a:a + part, :] = _dot_nt(q_ref[a:a + part, h * dh:(h + 1) * dh], kj)

    def softmax(j, slot, masked):
        def biased(h, r):
            s = s_sc[slot, h, r:r + sub, :] - c_ref[h, j]
            if masked:
                qpos = r + lax.broadcasted_iota(jnp.int32, (sub, blk), 0)
                kpos = lax.broadcasted_iota(jnp.int32, (sub, blk), 1)
                s = jnp.where(kpos <= qpos, s, NEG)
            return s

        for h in heads:
            for r in range(0, blk, sub):
                m_prev = m_sc[h, r:r + sub, :]
                m_new = jnp.maximum(m_prev, jnp.max(biased(h, r), axis=-1, keepdims=True))
                a_sc[slot, h, r:r + sub, :] = jnp.exp2(m_prev - m_new)
                m_sc[h, r:r + sub, :] = m_new
        for h in heads:
            for r in range(0, blk, sub):
                m_new = _lane_tile(m_sc[h, r:r + sub, :], blk // 128)
                p_sc[slot, h, r:r + sub, :] = jnp.exp2(biased(h, r) - m_new).astype(BF16)

    def values(j, slot):
        for h in heads:
            vj = vx_sc[h, rows_of(j), :]
            for a in range(0, blk, part):
                acc_sc[h, a:a + part, :] = (
                    acc_sc[h, a:a + part, :] * _lane_tile(a_sc[slot, h, a:a + part, :], 2)
                    + _dot(p_sc[slot, h, a:a + part, :], vj))

    def step(j, slot, masked, prefetch, prev):
        if prefetch:
            logits(j + 1, 1 - slot)
        if prev:
            values(j - 1, 1 - slot)
        softmax(j, slot, masked)

    def step_pair(t, _):
        step(2 * t + 1, 1, masked=False, prefetch=True, prev=True)
        step(2 * t + 2, 0, masked=False, prefetch=True, prev=True)
        return 0

    logits(0, 0)

    @pl.when(i == 0)
    def _():
        softmax(0, 0, masked=True)
        values(0, 0)

    @pl.when(i > 0)
    def _():
        step(0, 0, masked=False, prefetch=True, prev=False)
        lax.fori_loop(0, (i - 1) // 2, step_pair, 0)

        @pl.when(i % 2 == 0)
        def _():
            step(i - 1, 1, masked=False, prefetch=True, prev=True)
            step(i, 0, masked=True, prefetch=False, prev=True)
            values(i, 0)

        @pl.when(i % 2 == 1)
        def _():
            step(i, 1, masked=True, prefetch=False, prev=True)
            values(i, 1)

    for h in heads:
        o_ref[:, h * dh:(h + 1) * dh] = (acc_sc[h, :, :dh] / acc_sc[h, :, dh:]).astype(o_ref.dtype)


def _fox_attention(proj, c_blocks, blk, part, sub, hg):
    lp = proj.shape[0]
    nb = lp // blk
    dh = FOX_DH
    w = hg * dh
    kern = functools.partial(_fox_kernel, blk=blk, part=part, sub=sub, hg=hg)
    return pl.pallas_call(
        kern,
        grid=(FOX_HEADS // hg, nb),
        in_specs=[pl.BlockSpec((blk, w), lambda g, i: (i, COL_QA // w + g)),
                  pl.BlockSpec((lp, w), lambda g, i: (0, COL_KA // w + g)),
                  pl.BlockSpec((lp, w), lambda g, i: (0, COL_VA // w + g)),
                  pl.BlockSpec((hg, nb, 1, blk), lambda g, i: (g, 0, 0, 0))],
        out_specs=pl.BlockSpec((blk, w), lambda g, i: (i, g)),
        out_shape=jax.ShapeDtypeStruct((lp, FOX_WIDTH), BF16),
        scratch_shapes=[pltpu.VMEM((hg, lp, 2 * dh), BF16),
                        pltpu.VMEM((2, hg, blk, blk), F32),
                        pltpu.VMEM((2, hg, blk, blk), BF16),
                        pltpu.VMEM((2, hg, blk, 128), F32),
                        pltpu.VMEM((hg, blk, 128), F32),
                        pltpu.VMEM((hg, blk, 2 * dh), F32)],
        compiler_params=_params(("arbitrary", "arbitrary")),
        name="fox_attention",
    )(proj, proj, proj, c_blocks)


def _segment_reference_rows(b, n, sub8):
    c = b.shape[0]
    half = n // 2
    if n >= 16:
        pieces = [jnp.broadcast_to(b[m * n + half - 1:m * n + half, :], (n, b.shape[1]))
                  for m in range(c // n)]
        return pieces[0] if len(pieces) == 1 else jnp.concatenate(pieces, axis=0)
    b3 = b.reshape(c // 8, 8, b.shape[1])

    def bc(r):
        return jnp.broadcast_to(b3[:, r:r + 1, :], b3.shape).reshape(b.shape)

    out = bc(8 - n + half - 1)
    for start in range(8 - 2 * n, -1, -n):
        out = jnp.where(sub8 < start + n, bc(start + half - 1), out)
    return out


def _gla_kernel(q_ref, k_ref, v_ref, r_ref, sm_ref, wg_ref, bg_ref, ng_ref, o_ref, st_ref):
    c, dk, dv = GLA_CHUNK, GLA_DK, GLA_DV

    @pl.when(pl.program_id(0) == 0)
    def _():
        st_ref[...] = jnp.zeros(st_ref.shape, F32)

    t_idx = lax.broadcasted_iota(jnp.int32, (c, c), 0)
    s_idx = lax.broadcasted_iota(jnp.int32, (c, c), 1)
    tril = jnp.where(s_idx <= t_idx, 1.0, 0.0).astype(BF16)
    sub8 = t_idx & 7

    for ck, h in [(ck, h) for ck in range(q_ref.shape[0] // c) for h in range(GLA_HEADS)]:
        rows = slice(ck * c, (ck + 1) * c)
        q = q_ref[rows, h * dk:(h + 1) * dk].astype(F32)
        k = k_ref[rows, h * dk:(h + 1) * dk].astype(F32)
        v = v_ref[rows, h * dv:(h + 1) * dv]
        z = _dot(sm_ref[rows, :].astype(BF16), wg_ref[h].astype(BF16)) + bg_ref[h]
        g = _log_sigmoid(z) * (1.0 / GLA_TAU)
        hi, mid, lo = _split3(g)
        b = _dot(tril, hi) + _dot(tril, mid) + _dot(tril, lo)

        att = jnp.where(t_idx == s_idx, _dot_nt(q.astype(BF16), k.astype(BF16)), 0.0)
        n = c
        while n >= 2:
            upper = (t_idx & (n - 1)) >= n // 2
            ref = _segment_reference_rows(b, n, sub8)
            qs = jnp.where(upper, q * jnp.exp(jnp.minimum(b - ref, 0.0)), 0.0).astype(BF16)
            ks = jnp.where(upper, 0.0, k * jnp.exp(jnp.minimum(ref - b, 0.0))).astype(BF16)
            a = _dot_nt(qs, ks)
            if n < c:
                shift = n.bit_length() - 1
                a = jnp.where((t_idx >> shift) == (s_idx >> shift), a, 0.0)
            att = att + a
            n //= 2

        st = st_ref[h]
        o = _dot(att.astype(BF16), v) + _dot_nt((q * jnp.exp(b)).astype(BF16), st.astype(BF16))
        b_last = b[c - 1:c, :]
        ke = (k * jnp.exp(b_last - b)).astype(BF16)
        st_ref[h] = st * jnp.exp(b_last) + lax.dot_general(
            v, ke, (((0,), (0,)), ((), ())), preferred_element_type=F32)

        on = o * lax.rsqrt(jnp.mean(o * o, axis=-1, keepdims=True) + EPS) * ng_ref[h]
        r = r_ref[rows, h * dv:(h + 1) * dv].astype(F32)
        o_ref[rows, h * dv:(h + 1) * dv] = (r * _sigmoid(r) * on).astype(o_ref.dtype)


def _gla(proj, small, wg_pad, b_g, norm_g, rows):
    lp = proj.shape[0]
    c, dk, dv, nh = GLA_CHUNK, GLA_DK, GLA_DV, GLA_HEADS
    assert dk == c and rows % c == 0
    return pl.pallas_call(
        _gla_kernel,
        grid=(lp // rows,),
        in_specs=[pl.BlockSpec((rows, nh * dk), lambda i: (i, COL_QC // (nh * dk))),
                  pl.BlockSpec((rows, nh * dk), lambda i: (i, COL_KC // (nh * dk))),
                  pl.BlockSpec((rows, nh * dv), lambda i: (i, COL_VC // (nh * dv))),
                  pl.BlockSpec((rows, nh * dv), lambda i: (i, COL_RC // (nh * dv))),
                  pl.BlockSpec((rows, N_SMALL), lambda i: (i, 0)),
                  pl.BlockSpec((nh, N_SMALL, dk), lambda i: (0, 0, 0)),
                  pl.BlockSpec((nh, 1, dk), lambda i: (0, 0, 0)),
                  pl.BlockSpec((nh, 1, dv), lambda i: (0, 0, 0))],
        out_specs=pl.BlockSpec((rows, nh * dv), lambda i: (i, 0)),
        out_shape=jax.ShapeDtypeStruct((lp, nh * dv), BF16),
        scratch_shapes=[pltpu.VMEM((nh, dv, dk), F32)],
        compiler_params=_params(("arbitrary",)),
        name="gla",
    )(proj, proj, proj, proj, small, wg_pad, b_g, norm_g)


def _causal_conv3(ext, w_ref):
    m1 = pltpu.roll(ext, 1, 0)
    m2 = pltpu.roll(ext, 2, 0)
    return (ext[HALO:] * w_ref[2:3, :] + m1[HALO:] * w_ref[1:2, :] + m2[HALO:] * w_ref[0:1, :])


def _merge_kernel(oa_ref, scb_ref, scc_ref, sch_ref, scc_h_ref, sch_h_ref, oc_ref,
                  ga_ref, gb_ref, gc_ref, gbias_ref, cw_ref, wa_ref, wb_ref, wc_ref, o_ref):
    keep = (pl.program_id(0) > 0).astype(F32)
    p_halo = scc_h_ref[...].astype(F32) * sch_h_ref[...].astype(F32) * keep
    p = scc_ref[...].astype(F32) * sch_ref[...].astype(F32)
    conv = _causal_conv3(jnp.concatenate([p_halo, p], axis=0), cw_ref)
    ub = (scb_ref[...].astype(F32) * conv).astype(BF16)
    ya = _dot(oa_ref[...], wa_ref[...])
    yb = _dot(ub, wb_ref[...])
    yc = _dot(oc_ref[...], wc_ref[...])
    mix = (_sigmoid(ga_ref[...].astype(F32) + gbias_ref[0:1, :]) * ya
           + _sigmoid(gb_ref[...].astype(F32) + gbias_ref[1:2, :]) * yb
           + _sigmoid(gc_ref[...].astype(F32) + gbias_ref[2:3, :]) * yc)
    o_ref[...] = mix.astype(o_ref.dtype)


def _merge(proj, oa, oc, gate_b, conv_w, w_a_o, w_b_o, w_c_o, layer, tm):
    lp = proj.shape[0]
    d = D_MODEL
    ch = CONV_CH
    hb = tm // HALO

    def col(c0, width):
        return lambda i: (i, c0 // width)

    def halo(c0):
        return lambda i: (jnp.maximum(i * hb - 1, 0), c0 // ch)

    def const(shape):
        return pl.BlockSpec(shape, lambda i: (0, 0), pipeline_mode=pl.Buffered(1))

    def weight(k):
        return pl.BlockSpec((None, k, d), lambda i: (layer, 0, 0), pipeline_mode=pl.Buffered(1))

    return pl.pallas_call(
        _merge_kernel,
        grid=(lp // tm,),
        in_specs=[pl.BlockSpec((tm, FOX_WIDTH), lambda i: (i, 0)),
                  pl.BlockSpec((tm, ch), col(COL_SCB, ch)),
                  pl.BlockSpec((tm, ch), col(COL_SCC, ch)),
                  pl.BlockSpec((tm, ch), col(COL_SCH, ch)),
                  pl.BlockSpec((HALO, ch), halo(COL_SCC)),
                  pl.BlockSpec((HALO, ch), halo(COL_SCH)),
                  pl.BlockSpec((tm, GLA_HEADS * GLA_DV), lambda i: (i, 0)),
                  pl.BlockSpec((tm, d), col(COL_GA, d)),
                  pl.BlockSpec((tm, d), col(COL_GB, d)),
                  pl.BlockSpec((tm, d), col(COL_GC, d)),
                  const((3, d)),
                  const((3, ch)),
                  weight(FOX_WIDTH),
                  weight(ch),
                  weight(GLA_HEADS * GLA_DV)],
        out_specs=pl.BlockSpec((tm, d), lambda i: (i, 0)),
        out_shape=jax.ShapeDtypeStruct((lp, d), BF16),
        compiler_params=_params(("parallel",)),
        name="merge",
    )(oa, proj, proj, proj, proj, proj, oc, proj, proj, proj,
      gate_b.reshape(3, d), conv_w, w_a_o, w_b_o, w_c_o)


def _resid_matmul_kernel(x_ref, w_ref, h_ref, g_ref, ho_ref, no_ref, acc_ref):
    kk = pl.program_id(1)

    @pl.when(kk == 0)
    def _():
        acc_ref[...] = jnp.zeros(acc_ref.shape, F32)

    acc_ref[...] += _dot(x_ref[...], w_ref[...])

    @pl.when(kk == pl.num_programs(1) - 1)
    def _():
        hn = h_ref[...] + acc_ref[...]
        ho_ref[...] = hn
        y = hn * lax.rsqrt(jnp.mean(hn * hn, axis=-1, keepdims=True) + EPS)
        no_ref[...] = (y * g_ref[...]).astype(no_ref.dtype)


def _resid_matmul(x, w, layer, h, g, norm_dtype, tm, tk, name):
    lp, k = x.shape
    d = w.shape[2]
    return pl.pallas_call(
        _resid_matmul_kernel,
        grid=(lp // tm, k // tk),
        in_specs=[pl.BlockSpec((tm, tk), lambda i, kk: (i, kk)),
                  pl.BlockSpec((None, tk, d), lambda i, kk: (layer, kk, 0)),
                  pl.BlockSpec((tm, d), lambda i, kk: (i, 0)),
                  pl.BlockSpec((1, d), lambda i, kk: (0, 0))],
        out_specs=[pl.BlockSpec((tm, d), lambda i, kk: (i, 0)),
                   pl.BlockSpec((tm, d), lambda i, kk: (i, 0))],
        out_shape=[jax.ShapeDtypeStruct((lp, d), F32),
                   jax.ShapeDtypeStruct((lp, d), norm_dtype)],
        scratch_shapes=[pltpu.VMEM((tm, d), F32)],
        compiler_params=_params(("parallel", "arbitrary")),
        name=name,
    )(x, w, h, g.reshape(1, d))


def _matmul_resid_kernel(x_ref, w_ref, h_ref, o_ref):
    o_ref[...] = h_ref[...] + _dot(x_ref[...], w_ref[...])


def _matmul_resid(x, w, layer, h, tm, tn, name):
    lp, k = x.shape
    n = w.shape[2]
    return pl.pallas_call(
        _matmul_resid_kernel,
        grid=(lp // tm, n // tn),
        in_specs=[pl.BlockSpec((tm, k), lambda i, j: (i, 0)),
                  pl.BlockSpec((None, k, tn), lambda i, j: (layer, 0, j)),
                  pl.BlockSpec((tm, tn), lambda i, j: (i, j))],
        out_specs=pl.BlockSpec((tm, tn), lambda i, j: (i, j)),
        out_shape=jax.ShapeDtypeStruct((lp, n), F32),
        compiler_params=_params(("parallel", "arbitrary")),
        name=name,
    )(x, w, h)


def _rmsnorm_kernel(h_ref, g_ref, o_ref):
    x = h_ref[...]
    y = x * lax.rsqrt(jnp.mean(x * x, axis=-1, keepdims=True) + EPS)
    o_ref[...] = (y * g_ref[...]).astype(o_ref.dtype)


def _rmsnorm(h, g, tm):
    lp, d = h.shape
    return pl.pallas_call(
        _rmsnorm_kernel,
        grid=(lp // tm,),
        in_specs=[pl.BlockSpec((tm, d), lambda i: (i, 0)),
                  pl.BlockSpec((1, d), lambda i: (0, 0))],
        out_specs=pl.BlockSpec((tm, d), lambda i: (i, 0)),
        out_shape=jax.ShapeDtypeStruct((lp, d), BF16),
        compiler_params=_params(("parallel",)),
        name="rmsnorm",
    )(h, g.reshape(1, d))


def _final_norm_kernel(h_ref, hn_ref, g_ref, o_ref):
    x = jnp.concatenate([h_ref[N_META:, :], hn_ref[...]], axis=0)
    y = x * lax.rsqrt(jnp.mean(x * x, axis=-1, keepdims=True) + EPS)
    o_ref[...] = y * g_ref[...]


def _final_norm(h, g, seq, tm):
    lp, d = h.shape
    mb = tm // N_META
    last = lp // N_META - 1
    return pl.pallas_call(
        _final_norm_kernel,
        grid=(-(-seq // tm),),
        in_specs=[pl.BlockSpec((tm, d), lambda i: (i, 0)),
                  pl.BlockSpec((N_META, d), lambda i: (jnp.minimum((i + 1) * mb, last), 0)),
                  pl.BlockSpec((1, d), lambda i: (0, 0))],
        out_specs=pl.BlockSpec((tm, d), lambda i: (i, 0)),
        out_shape=jax.ShapeDtypeStruct((seq, d), F32),
        compiler_params=_params(("parallel",)),
        name="final_norm",
    )(h, h, g.reshape(1, d))


def _mlp_up_kernel(x_ref, xh_ref, wg_ref, wu_ref, cg_ref, cu_ref, o_ref, *, n_sub):
    rs = x_ref.shape[0] // n_sub
    keep = (pl.program_id(0) > 0).astype(BF16)
    wg = wg_ref[...].astype(BF16)
    wu = wu_ref[...].astype(BF16)
    for s in range(n_sub):
        if s == 0:
            xe = jnp.concatenate([xh_ref[...] * keep, x_ref[:rs, :]], axis=0)
        else:
            xe = x_ref[s * rs - HALO:(s + 1) * rs, :]
        ug = _causal_conv3(_dot(xe, wg), cg_ref)
        uu = _causal_conv3(_dot(xe, wu), cu_ref)
        o_ref[s * rs:(s + 1) * rs, :] = (ug * _sigmoid(ug) * uu).astype(o_ref.dtype)


def _mlp_up(xn, w_up, layer, conv_w, tm, tn):
    lp, d = xn.shape
    nj = D_FF // tn
    hb = tm // HALO
    n_sub = 4 if tm % (4 * HALO) == 0 else 1
    return pl.pallas_call(
        functools.partial(_mlp_up_kernel, n_sub=n_sub),
        grid=(lp // tm, nj),
        in_specs=[pl.BlockSpec((tm, d), lambda i, j: (i, 0)),
                  pl.BlockSpec((HALO, d), lambda i, j: (jnp.maximum(i * hb - 1, 0), 0)),
                  pl.BlockSpec((None, d, tn), lambda i, j: (layer, 0, j)),
                  pl.BlockSpec((None, d, tn), lambda i, j: (layer, 0, nj + j)),
                  pl.BlockSpec((3, tn), lambda i, j: (0, j)),
                  pl.BlockSpec((3, tn), lambda i, j: (0, nj + j))],
        out_specs=pl.BlockSpec((tm, tn), lambda i, j: (i, j)),
        out_shape=jax.ShapeDtypeStruct((lp, D_FF), BF16),
        compiler_params=_params(("parallel", "arbitrary")),
        name="mlp_up",
    )(xn, xn, w_up, w_up, conv_w, conv_w)


SRC_FA = 3 * FOX_WIDTH
SRC_MID = SRC_FA + FOX_HEADS
SRC_GLR = SRC_MID + 3 * CONV_CH + 2 * GLA_HEADS * GLA_DK + 2 * GLA_HEADS * GLA_DV
SRC_GATE = SRC_GLR + GLA_RANK
def _in_proj_kernel(x_ref, w_ref, s_ref, o_ref):
    w = w_ref[...].astype(BF16)
    o_ref[...] = (_dot_nt(x_ref[...], w) * s_ref[...]).astype(o_ref.dtype)


def _in_proj(xn, w_t, layer, colscale, tm, tn):
    lp, k = xn.shape
    assert COL_QA % tn == 0 and COL_SCB % tn == 0 and N_MAIN % tn == 0

    assert SRC_GATE % 8 == 0 and SRC_MID % 8 == 0 and tn % 8 == 0

    def src_row(j):
        c8 = j * (tn // 8)
        r8 = jnp.where(c8 < COL_QA // 8, SRC_GATE // 8 + c8,
                       jnp.where(c8 < COL_SCB // 8, c8 - COL_QA // 8,
                                 (SRC_MID - COL_SCB) // 8 + c8))
        return r8 * 8

    return pl.pallas_call(
        _in_proj_kernel,
        grid=(lp // tm, N_MAIN // tn),
        in_specs=[pl.BlockSpec((tm, k), lambda i, j: (i, 0)),
                  pl.BlockSpec((None, pl.Element(tn), pl.Element(k)),
                               lambda i, j: (layer, src_row(j), 0)),
                  pl.BlockSpec((1, tn), lambda i, j: (0, j))],
        out_specs=pl.BlockSpec((tm, tn), lambda i, j: (i, j)),
        out_shape=jax.ShapeDtypeStruct((lp, N_MAIN), BF16),
        compiler_params=_params(("parallel", "arbitrary")),
        name="in_proj",
    )(xn, w_t, colscale)


def _in_proj_small_kernel(x_ref, w_ref, o_ref):
    o_ref[...] = _dot_nt(x_ref[...], w_ref[...].astype(BF16))


def _in_proj_small(xn, w_small_t, layer, tm):
    lp, k = xn.shape
    return pl.pallas_call(
        _in_proj_small_kernel,
        grid=(lp // tm,),
        in_specs=[pl.BlockSpec((tm, k), lambda i: (i, 0)),
                  pl.BlockSpec((None, N_SMALL, k), lambda i: (layer, 0, 0))],
        out_specs=pl.BlockSpec((tm, N_SMALL), lambda i: (i, 0)),
        out_shape=jax.ShapeDtypeStruct((lp, N_SMALL), F32),
        compiler_params=_params(("parallel",)),
        name="in_proj_small",
    )(xn, w_small_t)


def _small_w_in(w_t):
    pad = jnp.zeros((w_t.shape[0], N_SMALL - FOX_HEADS - GLA_RANK, w_t.shape[2]), w_t.dtype)
    return jnp.concatenate([w_t[:, SRC_FA:SRC_MID], w_t[:, SRC_GLR:SRC_GATE], pad], axis=1)


def _main_colscale():
    s = jnp.ones((1, N_MAIN), F32)
    s = s.at[:, COL_QA:COL_QA + FOX_WIDTH].set(FOX_DH ** -0.5 * LOG2_E)
    s = s.at[:, COL_QC:COL_QC + GLA_HEADS * GLA_DK].set(GLA_DK ** -0.5)
    return s


def kernel(x, meta_tokens, norm1_g, w_in, fox_b_f, gate_b, conv_w, gla_w_g2, gla_b_g, gla_norm_g,
           w_a_o, w_b_o, w_c_o, w_o, norm2_g, w_up, mlp_conv_w, w_down, final_norm_g):
    assert x.shape[0] == 1 and x.shape[2] == D_MODEL
    depth = w_in.shape[0]
    seq = x.shape[1]
    l_real = N_META + seq
    lp = -(-l_real // ROW_TILE) * ROW_TILE
    tm = ROW_TILE
    blk = ROW_TILE

    h, xn = _embed(x[0], meta_tokens.astype(x.dtype), norm1_g[0], lp, tm)
    colscale = _main_colscale()
    w_t = jnp.swapaxes(w_in, 1, 2)
    w_small_t = _small_w_in(w_t)
    w_a_o, w_b_o, w_c_o, w_o, w_down = (w.astype(BF16) for w in (w_a_o, w_b_o, w_c_o, w_o, w_down))

    for l in range(depth):
        proj = _in_proj(xn, w_t, l, colscale, _row_tile(lp, 1664), 1024)
        small = _in_proj_small(xn, w_small_t, l, tm)

        f_blocks = small[:, :FOX_HEADS].T.reshape(FOX_HEADS, lp // 128, 128).transpose(1, 0, 2)
        c = _fox_cumsum(f_blocks, fox_b_f[l])
        c_blocks = c.transpose(1, 0, 2).reshape(FOX_HEADS, lp // blk, 1, blk)
        oa = _fox_attention(proj, c_blocks, blk, 320, 64, 2)

        wg = gla_w_g2[l].reshape(GLA_RANK, GLA_HEADS, GLA_DK).transpose(1, 0, 2)
        wg_pad = jnp.zeros((GLA_HEADS, N_SMALL, GLA_DK), F32)
        wg_pad = wg_pad.at[:, SMALL_GLR:SMALL_GLR + GLA_RANK, :].set(wg)
        oc = _gla(proj, small, wg_pad, gla_b_g[l].reshape(GLA_HEADS, 1, GLA_DK),
                  gla_norm_g[l].reshape(GLA_HEADS, 1, GLA_DV), ROW_TILE)

        mix = _merge(proj, oa, oc, gate_b[l], conv_w[l], w_a_o, w_b_o, w_c_o, l, 320)
        h, xn2 = _resid_matmul(mix, w_o, l, h, norm2_g[l], BF16, 320, D_MODEL, "out_proj")

        act = _mlp_up(xn2, w_up, l, mlp_conv_w[l], _row_tile(lp, 1664), 512)
        h = _matmul_resid(act, w_down, l, h, _row_tile(lp, 832, 64), 512, "mlp_down")
        if l + 1 < depth:
            xn = _rmsnorm(h, norm1_g[l + 1], tm)
    return _final_norm(h, final_norm_g, seq, tm)[None]
```

```python
import functools

import jax
import jax.numpy as jnp
from jax import lax
from jax.experimental import pallas as pl
from jax.experimental.pallas import tpu as pltpu

F32 = jnp.float32
BF16 = jnp.bfloat16

D_MODEL = 2048
N_META = 16
EPS = 1e-6
NEG = -1e30
LOG2_E = 1.4426950408889634

FOX_HEADS = 8
FOX_DH = D_MODEL // 16
FOX_WIDTH = FOX_HEADS * FOX_DH
CONV_CH = D_MODEL // 2
GLA_HEADS = 4
GLA_DK = D_MODEL // 16
GLA_DV = D_MODEL // 8
GLA_RANK = 16
GLA_TAU = 16.0
GLA_CHUNK = 128
D_FF = D_MODEL * 11 // 4

COL_GA = 0
COL_GB = COL_GA + D_MODEL
COL_GC = COL_GB + D_MODEL
COL_QA = COL_GC + D_MODEL
COL_KA = COL_QA + FOX_WIDTH
COL_VA = COL_KA + FOX_WIDTH
COL_SCB = COL_VA + FOX_WIDTH
COL_SCC = COL_SCB + CONV_CH
COL_SCH = COL_SCC + CONV_CH
COL_QC = COL_SCH + CONV_CH
COL_KC = COL_QC + GLA_HEADS * GLA_DK
COL_VC = COL_KC + GLA_HEADS * GLA_DK
COL_RC = COL_VC + GLA_HEADS * GLA_DV
N_MAIN = COL_RC + GLA_HEADS * GLA_DV
N_SMALL = 128
SMALL_GLR = FOX_HEADS

SRC_FA = 3 * FOX_WIDTH
SRC_MID = SRC_FA + FOX_HEADS
SRC_GLR = SRC_MID + 3 * CONV_CH + 2 * GLA_HEADS * GLA_DK + 2 * GLA_HEADS * GLA_DV
SRC_GATE = SRC_GLR + GLA_RANK

ROW_TILE = 640
HALO = 16
VMEM_LIMIT = 56 * 1024 * 1024


def _row_tile(lp, cap, granule=128):
    return max(t for t in range(granule, min(cap, lp) + 1, granule) if lp % t == 0)


def _params(sem, vmem=VMEM_LIMIT):
    return pltpu.CompilerParams(dimension_semantics=sem, vmem_limit_bytes=vmem)


def _sigmoid(x):
    return 1.0 / (1.0 + jnp.exp(-x))


def _log_sigmoid(x):
    return jnp.minimum(x, 0.0) - jnp.log(1.0 + jnp.exp(-jnp.abs(x)))


def _dot(a, b):
    return jnp.dot(a, b, preferred_element_type=F32)


def _dot_nt(a, b):
    return lax.dot_general(a, b, (((1,), (1,)), ((), ())), preferred_element_type=F32)


def _split3(x):
    hi = x.astype(BF16)
    r1 = x - hi.astype(F32)
    mid = r1.astype(BF16)
    lo = (r1 - mid.astype(F32)).astype(BF16)
    return hi, mid, lo


def _embed_kernel(x_ref, xp_ref, meta_ref, g_ref, h_ref, n_ref, *, l_real):
    i = pl.program_id(0)
    tm = h_ref.shape[0]
    top = jnp.where(i == 0, meta_ref[...], xp_ref[...])
    hb = jnp.concatenate([top, x_ref[:tm - N_META, :]], axis=0)
    row = i * tm + lax.broadcasted_iota(jnp.int32, (tm, 1), 0)
    hb = jnp.where(row < l_real, hb, 0.0)
    h_ref[...] = hb
    y = hb * lax.rsqrt(jnp.mean(hb * hb, axis=-1, keepdims=True) + EPS)
    n_ref[...] = (y * g_ref[...]).astype(n_ref.dtype)


def _embed(x2, meta, g, lp, tm):
    seq, d = x2.shape
    last_x_block = (seq - 1) // tm
    mb = tm // N_META
    return pl.pallas_call(
        functools.partial(_embed_kernel, l_real=N_META + seq),
        grid=(lp // tm,),
        in_specs=[pl.BlockSpec((tm, d), lambda i: (jnp.minimum(i, last_x_block), 0)),
                  pl.BlockSpec((N_META, d), lambda i: (jnp.maximum(i * mb - 1, 0), 0)),
                  pl.BlockSpec((N_META, d), lambda i: (0, 0)),
                  pl.BlockSpec((1, d), lambda i: (0, 0))],
        out_specs=[pl.BlockSpec((tm, d), lambda i: (i, 0)),
                   pl.BlockSpec((tm, d), lambda i: (i, 0))],
        out_shape=[jax.ShapeDtypeStruct((lp, d), F32),
                   jax.ShapeDtypeStruct((lp, d), BF16)],
        compiler_params=_params(("parallel",)),
        name="embed",
    )(x2, x2, meta, g.reshape(1, d))


def _fox_cumsum_kernel(f_ref, b_ref, c_ref):
    nblk = f_ref.shape[0]
    row = lax.broadcasted_iota(jnp.int32, (128, 128), 0)
    col = lax.broadcasted_iota(jnp.int32, (128, 128), 1)
    triu = jnp.where(row <= col, 1.0, 0.0).astype(BF16)

    def body(j, carry):
        logf = _log_sigmoid(f_ref[j] + b_ref[...]) * LOG2_E
        hi, mid, lo = _split3(logf)
        cs = _dot(hi, triu) + _dot(mid, triu) + _dot(lo, triu) + carry
        c_ref[j] = cs
        return cs[:, 127:128]

    lax.fori_loop(0, nblk, body, jnp.zeros((f_ref.shape[1], 1), F32))


def _fox_cumsum(f_blocks, b_f):
    nblk, nh, _ = f_blocks.shape
    return pl.pallas_call(
        _fox_cumsum_kernel,
        out_shape=jax.ShapeDtypeStruct((nblk, nh, 128), F32),
        name="fox_cumsum",
    )(f_blocks, b_f.reshape(nh, 1))


def _lane_tile(x, n):
    return jnp.concatenate([x] * n, axis=1)


def _fox_kernel(q_ref, k_ref, v_ref, c_ref, o_ref, vx_sc, s_sc, p_sc, a_sc, m_sc, acc_sc,
                *, blk, part, sub, hg):
    i = pl.program_id(1)
    dh = q_ref.shape[1] // hg
    heads = range(hg)

    @pl.when(i == 0)
    def _():
        for h in heads:
            vx_sc[h, :, :dh] = v_ref[:, h * dh:(h + 1) * dh]
            vx_sc[h, :, dh:] = jnp.ones((vx_sc.shape[1], dh), BF16)

    m_sc[...] = jnp.full(m_sc.shape, NEG, F32)
    acc_sc[...] = jnp.zeros(acc_sc.shape, F32)

    def rows_of(j):
        return pl.ds(pl.multiple_of(j * blk, blk), blk)

    def logits(j, slot):
        for h in heads:
            kj = k_ref[rows_of(j), h * dh:(h + 1) * dh]
            for a in range(0, blk, part):
                s_sc[slot, h, a:a + part, :] = _dot_nt(q_ref[a:a + part, h * dh:(h + 1) * dh], kj)

    def softmax(j, slot, masked):
        def biased(h, r):
            s = s_sc[slot, h, r:r + sub, :] - c_ref[h, j]
            if masked:
                qpos = r + lax.broadcasted_iota(jnp.int32, (sub, blk), 0)
                kpos = lax.broadcasted_iota(jnp.int32, (sub, blk), 1)
                s = jnp.where(kpos <= qpos, s, NEG)
            return s

        for h in heads:
            for r in range(0, blk, sub):
                m_prev = m_sc[h, r:r + sub, :]
                m_new = jnp.maximum(m_prev, jnp.max(biased(h, r), axis=-1, keepdims=True))
                a_sc[slot, h, r:r + sub, :] = jnp.exp2(m_prev - m_new)
                m_sc[h, r:r + sub, :] = m_new
        for h in heads:
            for r in range(0, blk, sub):
                m_new = _lane_tile(m_sc[h, r:r + sub, :], blk // 128)
                p_sc[slot, h, r:r + sub, :] = jnp.exp2(biased(h, r) - m_new).astype(BF16)

    def values(j, slot):
        for h in heads:
            vj = vx_sc[h, rows_of(j), :]
            for a in range(0, blk, part):
                acc_sc[h, a:a + part, :] = (
                    acc_sc[h, a:a + part, :] * _lane_tile(a_sc[slot, h, a:a + part, :], 2)
                    + _dot(p_sc[slot, h, a:a + part, :], vj))

    def step(j, slot, masked, prefetch, prev):
        if prefetch:
            logits(j + 1, 1 - slot)
        if prev:
            values(j - 1, 1 - slot)
        softmax(j, slot, masked)

    def step_pair(t, _):
        step(2 * t + 1, 1, masked=False, prefetch=True, prev=True)
        step(2 * t + 2, 0, masked=False, prefetch=True, prev=True)
        return 0

    logits(0, 0)

    @pl.when(i == 0)
    def _():
        softmax(0, 0, masked=True)
        values(0, 0)

    @pl.when(i > 0)
    def _():
        step(0, 0, masked=False, prefetch=True, prev=False)
        lax.fori_loop(0, (i - 1) // 2, step_pair, 0)

        @pl.when(i % 2 == 0)
        def _():
            step(i - 1, 1, masked=False, prefetch=True, prev=True)
            step(i, 0, masked=True, prefetch=False, prev=True)
            values(i, 0)

        @pl.when(i % 2 == 1)
        def _():
            step(i, 1, masked=True, prefetch=False, prev=True)
            values(i, 1)

    for h in heads:
        o_ref[:, h * dh:(h + 1) * dh] = (acc_sc[h, :, :dh] / acc_sc[h, :, dh:]).astype(o_ref.dtype)


def _fox_attention(proj, c_blocks, blk, part, sub, hg):
    lp = proj.shape[0]
    nb = lp // blk
    dh = FOX_DH
    w = hg * dh
    kern = functools.partial(_fox_kernel, blk=blk, part=part, sub=sub, hg=hg)
    return pl.pallas_call(
        kern,
        grid=(FOX_HEADS // hg, nb),
        in_specs=[pl.BlockSpec((blk, w), lambda g, i: (i, COL_QA // w + g)),
                  pl.BlockSpec((lp, w), lambda g, i: (0, COL_KA // w + g)),
                  pl.BlockSpec((lp, w), lambda g, i: (0, COL_VA // w + g)),
                  pl.BlockSpec((hg, nb, 1, blk), lambda g, i: (g, 0, 0, 0))],
        out_specs=pl.BlockSpec((blk, w), lambda g, i: (i, g)),
        out_shape=jax.ShapeDtypeStruct((lp, FOX_WIDTH), BF16),
        scratch_shapes=[pltpu.VMEM((hg, lp, 2 * dh), BF16),
                        pltpu.VMEM((2, hg, blk, blk), F32),
                        pltpu.VMEM((2, hg, blk, blk), BF16),
                        pltpu.VMEM((2, hg, blk, 128), F32),
                        pltpu.VMEM((hg, blk, 128), F32),
                        pltpu.VMEM((hg, blk, 2 * dh), F32)],
        compiler_params=_params(("arbitrary", "arbitrary")),
        name="fox_attention",
    )(proj, proj, proj, c_blocks)


def _segment_reference_rows(b, n, sub8):
    c = b.shape[0]
    half = n // 2
    if n >= 16:
        pieces = [jnp.broadcast_to(b[m * n + half - 1:m * n + half, :], (n, b.shape[1]))
                  for m in range(c // n)]
        return pieces[0] if len(pieces) == 1 else jnp.concatenate(pieces, axis=0)
    b3 = b.reshape(c // 8, 8, b.shape[1])

    def bc(r):
        return jnp.broadcast_to(b3[:, r:r + 1, :], b3.shape).reshape(b.shape)

    out = bc(8 - n + half - 1)
    for start in range(8 - 2 * n, -1, -n):
        out = jnp.where(sub8 < start + n, bc(start + half - 1), out)
    return out


def _gla_kernel(q_ref, k_ref, v_ref, r_ref, sm_ref, wg_ref, bg_ref, ng_ref, o_ref, st_ref):
    c, dk, dv = GLA_CHUNK, GLA_DK, GLA_DV

    @pl.when(pl.program_id(0) == 0)
    def _():
        st_ref[...] = jnp.zeros(st_ref.shape, F32)

    t_idx = lax.broadcasted_iota(jnp.int32, (c, c), 0)
    s_idx = lax.broadcasted_iota(jnp.int32, (c, c), 1)
    tril = jnp.where(s_idx <= t_idx, 1.0, 0.0).astype(BF16)
    sub8 = t_idx & 7

    for ck, h in [(ck, h) for ck in range(q_ref.shape[0] // c) for h in range(GLA_HEADS)]:
        rows = slice(ck * c, (ck + 1) * c)
        q = q_ref[rows, h * dk:(h + 1) * dk].astype(F32)
        k = k_ref[rows, h * dk:(h + 1) * dk].astype(F32)
        v = v_ref[rows, h * dv:(h + 1) * dv]
        z = _dot(sm_ref[rows, :].astype(BF16), wg_ref[h].astype(BF16)) + bg_ref[h]
        g = _log_sigmoid(z) * (1.0 / GLA_TAU)
        hi, mid, lo = _split3(g)
        b = _dot(tril, hi) + _dot(tril, mid) + _dot(tril, lo)

        att = jnp.where(t_idx == s_idx, _dot_nt(q.astype(BF16), k.astype(BF16)), 0.0)
        n = c
        while n >= 2:
            upper = (t_idx & (n - 1)) >= n // 2
            ref = _segment_reference_rows(b, n, sub8)
            qs = jnp.where(upper, q * jnp.exp(jnp.minimum(b - ref, 0.0)), 0.0).astype(BF16)
            ks = jnp.where(upper, 0.0, k * jnp.exp(jnp.minimum(ref - b, 0.0))).astype(BF16)
            a = _dot_nt(qs, ks)
            if n < c:
                shift = n.bit_length() - 1
                a = jnp.where((t_idx >> shift) == (s_idx >> shift), a, 0.0)
            att = att + a
            n //= 2

        st = st_ref[h]
        o = _dot(att.astype(BF16), v) + _dot_nt((q * jnp.exp(b)).astype(BF16), st.astype(BF16))
        b_last = b[c - 1:c, :]
        ke = (k * jnp.exp(b_last - b)).astype(BF16)
        st_ref[h] = st * jnp.exp(b_last) + lax.dot_general(
            v, ke, (((0,), (0,)), ((), ())), preferred_element_type=F32)

        on = o * lax.rsqrt(jnp.mean(o * o, axis=-1, keepdims=True) + EPS) * ng_ref[h]
        r = r_ref[rows, h * dv:(h + 1) * dv].astype(F32)
        o_ref[rows, h * dv:(h + 1) * dv] = (r * _sigmoid(r) * on).astype(o_ref.dtype)


def _gla(proj, small, wg_pad, b_g, norm_g, rows):
    lp = proj.shape[0]
    c, dk, dv, nh = GLA_CHUNK, GLA_DK, GLA_DV, GLA_HEADS
    assert dk == c and rows % c == 0
    return pl.pallas_call(
        _gla_kernel,
        grid=(lp // rows,),
        in_specs=[pl.BlockSpec((rows, nh * dk), lambda i: (i, COL_QC // (nh * dk))),
                  pl.BlockSpec((rows, nh * dk), lambda i: (i, COL_KC // (nh * dk))),
                  pl.BlockSpec((rows, nh * dv), lambda i: (i, COL_VC // (nh * dv))),
                  pl.BlockSpec((rows, nh * dv), lambda i: (i, COL_RC // (nh * dv))),
                  pl.BlockSpec((rows, N_SMALL), lambda i: (i, 0)),
                  pl.BlockSpec((nh, N_SMALL, dk), lambda i: (0, 0, 0)),
                  pl.BlockSpec((nh, 1, dk), lambda i: (0, 0, 0)),
                  pl.BlockSpec((nh, 1, dv), lambda i: (0, 0, 0))],
        out_specs=pl.BlockSpec((rows, nh * dv), lambda i: (i, 0)),
        out_shape=jax.ShapeDtypeStruct((lp, nh * dv), BF16),
        scratch_shapes=[pltpu.VMEM((nh, dv, dk), F32)],
        compiler_params=_params(("arbitrary",)),
        name="gla",
    )(proj, proj, proj, proj, small, wg_pad, b_g, norm_g)


def _causal_conv3(ext, w_ref):
    m1 = pltpu.roll(ext, 1, 0)
    m2 = pltpu.roll(ext, 2, 0)
    return (ext[HALO:] * w_ref[2:3, :] + m1[HALO:] * w_ref[1:2, :] + m2[HALO:] * w_ref[0:1, :])


def _merge_kernel(oa_ref, scb_ref, scc_ref, sch_ref, scc_h_ref, sch_h_ref, oc_ref,
                  ga_ref, gb_ref, gc_ref, gbias_ref, cw_ref, wa_ref, wb_ref, wc_ref, o_ref):
    keep = (pl.program_id(0) > 0).astype(F32)
    p_halo = scc_h_ref[...].astype(F32) * sch_h_ref[...].astype(F32) * keep
    p = scc_ref[...].astype(F32) * sch_ref[...].astype(F32)
    conv = _causal_conv3(jnp.concatenate([p_halo, p], axis=0), cw_ref)
    ub = (scb_ref[...].astype(F32) * conv).astype(BF16)
    ya = _dot(oa_ref[...], wa_ref[...])
    yb = _dot(ub, wb_ref[...])
    yc = _dot(oc_ref[...], wc_ref[...])
    mix = (_sigmoid(ga_ref[...].astype(F32) + gbias_ref[0:1, :]) * ya
           + _sigmoid(gb_ref[...].astype(F32) + gbias_ref[1:2, :]) * yb
           + _sigmoid(gc_ref[...].astype(F32) + gbias_ref[2:3, :]) * yc)
    o_ref[...] = mix.astype(o_ref.dtype)


def _merge(proj, oa, oc, gate_b, conv_w, w_a_o, w_b_o, w_c_o, layer, tm):
    lp = proj.shape[0]
    d = D_MODEL
    ch = CONV_CH
    hb = tm // HALO

    def col(c0, width):
        return lambda i: (i, c0 // width)

    def halo(c0):
        return lambda i: (jnp.maximum(i * hb - 1, 0), c0 // ch)

    def const(shape):
        return pl.BlockSpec(shape, lambda i: (0, 0), pipeline_mode=pl.Buffered(1))

    def weight(k):
        return pl.BlockSpec((None, k, d), lambda i: (layer, 0, 0), pipeline_mode=pl.Buffered(1))

    return pl.pallas_call(
        _merge_kernel,
        grid=(lp // tm,),
        in_specs=[pl.BlockSpec((tm, FOX_WIDTH), lambda i: (i, 0)),
                  pl.BlockSpec((tm, ch), col(COL_SCB, ch)),
                  pl.BlockSpec((tm, ch), col(COL_SCC, ch)),
                  pl.BlockSpec((tm, ch), col(COL_SCH, ch)),
                  pl.BlockSpec((HALO, ch), halo(COL_SCC)),
                  pl.BlockSpec((HALO, ch), halo(COL_SCH)),
                  pl.BlockSpec((tm, GLA_HEADS * GLA_DV), lambda i: (i, 0)),
                  pl.BlockSpec((tm, d), col(COL_GA, d)),
                  pl.BlockSpec((tm, d), col(COL_GB, d)),
                  pl.BlockSpec((tm, d), col(COL_GC, d)),
                  const((3, d)),
                  const((3, ch)),
                  weight(FOX_WIDTH),
                  weight(ch),
                  weight(GLA_HEADS * GLA_DV)],
        out_specs=pl.BlockSpec((tm, d), lambda i: (i, 0)),
        out_shape=jax.ShapeDtypeStruct((lp, d), BF16),
        compiler_params=_params(("parallel",)),
        name="merge",
    )(oa, proj, proj, proj, proj, proj, oc, proj, proj, proj,
      gate_b.reshape(3, d), conv_w, w_a_o, w_b_o, w_c_o)


def _resid_matmul_kernel(x_ref, w_ref, h_ref, g_ref, ho_ref, no_ref):
    hn = h_ref[...] + _dot(x_ref[...], w_ref[...])
    ho_ref[...] = hn
    y = hn * lax.rsqrt(jnp.mean(hn * hn, axis=-1, keepdims=True) + EPS)
    no_ref[...] = (y * g_ref[...]).astype(no_ref.dtype)


def _resid_matmul(x, w, layer, h, g, norm_dtype, tm, name):
    lp, k = x.shape
    d = w.shape[2]
    return pl.pallas_call(
        _resid_matmul_kernel,
        grid=(lp // tm,),
        in_specs=[pl.BlockSpec((tm, k), lambda i: (i, 0)),
                  pl.BlockSpec((None, k, d), lambda i: (layer, 0, 0), pipeline_mode=pl.Buffered(1)),
                  pl.BlockSpec((tm, d), lambda i: (i, 0)),
                  pl.BlockSpec((1, d), lambda i: (0, 0))],
        out_specs=[pl.BlockSpec((tm, d), lambda i: (i, 0)),
                   pl.BlockSpec((tm, d), lambda i: (i, 0))],
        out_shape=[jax.ShapeDtypeStruct((lp, d), F32),
                   jax.ShapeDtypeStruct((lp, d), norm_dtype)],
        compiler_params=_params(("parallel",)),
        name=name,
    )(x, w, h, g.reshape(1, d))


def _matmul_resid_kernel(x_ref, w_ref, h_ref, o_ref):
    o_ref[...] = h_ref[...] + _dot(x_ref[...], w_ref[...])


def _matmul_resid(x, w, layer, h, tm, tn, name):
    lp, k = x.shape
    n = w.shape[2]
    return pl.pallas_call(
        _matmul_resid_kernel,
        grid=(lp // tm, n // tn),
        in_specs=[pl.BlockSpec((tm, k), lambda i, j: (i, 0)),
                  pl.BlockSpec((None, k, tn), lambda i, j: (layer, 0, j)),
                  pl.BlockSpec((tm, tn), lambda i, j: (i, j))],
        out_specs=pl.BlockSpec((tm, tn), lambda i, j: (i, j)),
        out_shape=jax.ShapeDtypeStruct((lp, n), F32),
        compiler_params=_params(("parallel", "arbitrary")),
        name=name,
    )(x, w, h)


def _rmsnorm_kernel(h_ref, g_ref, o_ref):
    x = h_ref[...]
    y = x * lax.rsqrt(jnp.mean(x * x, axis=-1, keepdims=True) + EPS)
    o_ref[...] = (y * g_ref[...]).astype(o_ref.dtype)


def _rmsnorm(h, g, tm):
    lp, d = h.shape
    return pl.pallas_call(
        _rmsnorm_kernel,
        grid=(lp // tm,),
        in_specs=[pl.BlockSpec((tm, d), lambda i: (i, 0)),
                  pl.BlockSpec((1, d), lambda i: (0, 0))],
        out_specs=pl.BlockSpec((tm, d), lambda i: (i, 0)),
        out_shape=jax.ShapeDtypeStruct((lp, d), BF16),
        compiler_params=_params(("parallel",)),
        name="rmsnorm",
    )(h, g.reshape(1, d))


def _final_norm_kernel(h_ref, hn_ref, g_ref, o_ref):
    x = jnp.concatenate([h_ref[N_META:, :], hn_ref[...]], axis=0)
    y = x * lax.rsqrt(jnp.mean(x * x, axis=-1, keepdims=True) + EPS)
    o_ref[...] = y * g_ref[...]


def _final_norm(h, g, seq, tm):
    lp, d = h.shape
    mb = tm // N_META
    last = lp // N_META - 1
    return pl.pallas_call(
        _final_norm_kernel,
        grid=(-(-seq // tm),),
        in_specs=[pl.BlockSpec((tm, d), lambda i: (i, 0)),
                  pl.BlockSpec((N_META, d), lambda i: (jnp.minimum((i + 1) * mb, last), 0)),
                  pl.BlockSpec((1, d), lambda i: (0, 0))],
        out_specs=pl.BlockSpec((tm, d), lambda i: (i, 0)),
        out_shape=jax.ShapeDtypeStruct((seq, d), F32),
        compiler_params=_params(("parallel",)),
        name="final_norm",
    )(h, h, g.reshape(1, d))


def _mlp_up_kernel(x_ref, xh_ref, wg_ref, wu_ref, cg_ref, cu_ref, o_ref, *, n_sub):
    rs = x_ref.shape[0] // n_sub
    keep = (pl.program_id(0) > 0).astype(BF16)
    wg = wg_ref[...].astype(BF16)
    wu = wu_ref[...].astype(BF16)
    for s in range(n_sub):
        if s == 0:
            xe = jnp.concatenate([xh_ref[...] * keep, x_ref[:rs, :]], axis=0)
        else:
            xe = x_ref[s * rs - HALO:(s + 1) * rs, :]
        ug = _causal_conv3(_dot(xe, wg), cg_ref)
        uu = _causal_conv3(_dot(xe, wu), cu_ref)
        o_ref[s * rs:(s + 1) * rs, :] = (ug * _sigmoid(ug) * uu).astype(o_ref.dtype)


def _mlp_up(xn, w_up, layer, conv_w, tm, tn):
    lp, d = xn.shape
    nj = D_FF // tn
    hb = tm // HALO
    n_sub = 4 if tm % (4 * HALO) == 0 else 1
    return pl.pallas_call(
        functools.partial(_mlp_up_kernel, n_sub=n_sub),
        grid=(lp // tm, nj),
        in_specs=[pl.BlockSpec((tm, d), lambda i, j: (i, 0)),
                  pl.BlockSpec((HALO, d), lambda i, j: (jnp.maximum(i * hb - 1, 0), 0)),
                  pl.BlockSpec((None, d, tn), lambda i, j: (layer, 0, j)),
                  pl.BlockSpec((None, d, tn), lambda i, j: (layer, 0, nj + j)),
                  pl.BlockSpec((3, tn), lambda i, j: (0, j)),
                  pl.BlockSpec((3, tn), lambda i, j: (0, nj + j))],
        out_specs=pl.BlockSpec((tm, tn), lambda i, j: (i, j)),
        out_shape=jax.ShapeDtypeStruct((lp, D_FF), BF16),
        compiler_params=_params(("parallel", "arbitrary")),
        name="mlp_up",
    )(xn, xn, w_up, w_up, conv_w, conv_w)


def _in_proj_kernel(x_ref, w_ref, s_ref, o_ref):
    w = w_ref[...].astype(BF16)
    o_ref[...] = (_dot_nt(x_ref[...], w) * s_ref[...]).astype(o_ref.dtype)


def _in_proj(xn, w_t, layer, colscale, tm, tn):
    lp, k = xn.shape
    assert COL_QA % tn == 0 and COL_SCB % tn == 0 and N_MAIN % tn == 0
    assert SRC_GATE % 8 == 0 and SRC_MID % 8 == 0 and tn % 8 == 0

    def src_row(j):
        c8 = j * (tn // 8)
        r8 = jnp.where(c8 < COL_QA // 8, SRC_GATE // 8 + c8,
                       jnp.where(c8 < COL_SCB // 8, c8 - COL_QA // 8,
                                 (SRC_MID - COL_SCB) // 8 + c8))
        return r8 * 8

    return pl.pallas_call(
        _in_proj_kernel,
        grid=(lp // tm, N_MAIN // tn),
        in_specs=[pl.BlockSpec((tm, k), lambda i, j: (i, 0)),
                  pl.BlockSpec((None, pl.Element(tn), pl.Element(k)),
                               lambda i, j: (layer, src_row(j), 0)),
                  pl.BlockSpec((1, tn), lambda i, j: (0, j))],
        out_specs=pl.BlockSpec((tm, tn), lambda i, j: (i, j)),
        out_shape=jax.ShapeDtypeStruct((lp, N_MAIN), BF16),
        compiler_params=_params(("parallel", "arbitrary")),
        name="in_proj",
    )(xn, w_t, colscale)


def _in_proj_small_kernel(x_ref, w_ref, o_ref):
    o_ref[...] = _dot_nt(x_ref[...], w_ref[...].astype(BF16))


def _in_proj_small(xn, w_small_t, layer, tm):
    lp, k = xn.shape
    return pl.pallas_call(
        _in_proj_small_kernel,
        grid=(lp // tm,),
        in_specs=[pl.BlockSpec((tm, k), lambda i: (i, 0)),
                  pl.BlockSpec((None, N_SMALL, k), lambda i: (layer, 0, 0))],
        out_specs=pl.BlockSpec((tm, N_SMALL), lambda i: (i, 0)),
        out_shape=jax.ShapeDtypeStruct((lp, N_SMALL), F32),
        compiler_params=_params(("parallel",)),
        name="in_proj_small",
    )(xn, w_small_t)


def _small_w_in(w_t):
    pad = jnp.zeros((w_t.shape[0], N_SMALL - FOX_HEADS - GLA_RANK, w_t.shape[2]), w_t.dtype)
    return jnp.concatenate([w_t[:, SRC_FA:SRC_MID], w_t[:, SRC_GLR:SRC_GATE], pad], axis=1)


def _main_colscale():
    s = jnp.ones((1, N_MAIN), F32)
    s = s.at[:, COL_QA:COL_QA + FOX_WIDTH].set(FOX_DH ** -0.5 * LOG2_E)
    s = s.at[:, COL_QC:COL_QC + GLA_HEADS * GLA_DK].set(GLA_DK ** -0.5)
    return s


def kernel(x, meta_tokens, norm1_g, w_in, fox_b_f, gate_b, conv_w, gla_w_g2, gla_b_g, gla_norm_g,
           w_a_o, w_b_o, w_c_o, w_o, norm2_g, w_up, mlp_conv_w, w_down, final_norm_g):
    assert x.shape[0] == 1 and x.shape[2] == D_MODEL
    depth = w_in.shape[0]
    seq = x.shape[1]
    l_real = N_META + seq
    lp = -(-l_real // ROW_TILE) * ROW_TILE
    tm = ROW_TILE
    blk = ROW_TILE

    h, xn = _embed(x[0], meta_tokens.astype(x.dtype), norm1_g[0], lp, tm)
    colscale = _main_colscale()
    w_t = jnp.swapaxes(w_in, 1, 2)
    w_small_t = _small_w_in(w_t)
    w_a_o, w_b_o, w_c_o, w_o, w_down = (w.astype(BF16) for w in (w_a_o, w_b_o, w_c_o, w_o, w_down))

    for l in range(depth):
        proj = _in_proj(xn, w_t, l, colscale, _row_tile(lp, 1664), 1024)
        small = _in_proj_small(xn, w_small_t, l, tm)

        f_blocks = small[:, :FOX_HEADS].T.reshape(FOX_HEADS, lp // 128, 128).transpose(1, 0, 2)
        c = _fox_cumsum(f_blocks, fox_b_f[l])
        c_blocks = c.transpose(1, 0, 2).reshape(FOX_HEADS, lp // blk, 1, blk)
        oa = _fox_attention(proj, c_blocks, blk, 320, 64, 2)

        wg = gla_w_g2[l].reshape(GLA_RANK, GLA_HEADS, GLA_DK).transpose(1, 0, 2)
        wg_pad = jnp.zeros((GLA_HEADS, N_SMALL, GLA_DK), F32)
        wg_pad = wg_pad.at[:, SMALL_GLR:SMALL_GLR + GLA_RANK, :].set(wg)
        oc = _gla(proj, small, wg_pad, gla_b_g[l].reshape(GLA_HEADS, 1, GLA_DK),
                  gla_norm_g[l].reshape(GLA_HEADS, 1, GLA_DV), ROW_TILE)

        mix = _merge(proj, oa, oc, gate_b[l], conv_w[l], w_a_o, w_b_o, w_c_o, l, 320)
        h, xn2 = _resid_matmul(mix, w_o, l, h, norm2_g[l], BF16, 640, "out_proj")

        act = _mlp_up(xn2, w_up, l, mlp_conv_w[l], _row_tile(lp, 1664), 512)
        h = _matmul_resid(act, w_down, l, h, _row_tile(lp, 832, 64), 512, "mlp_down")
        if l + 1 < depth:
            xn = _rmsnorm(h, norm1_g[l + 1], tm)
    return _final_norm(h, final_norm_g, seq, tm)[None]
```

```python
import functools

import jax
import jax.numpy as jnp
from jax import lax
from jax.experimental import pallas as pl
from jax.experimental.pallas import tpu as pltpu

F32 = jnp.float32
BF16 = jnp.bfloat16

D_MODEL = 2048
N_META = 16
EPS = 1e-6
NEG = -1e30
LOG2_E = 1.4426950408889634

FOX_HEADS = 8
FOX_DH = D_MODEL // 16
FOX_WIDTH = FOX_HEADS * FOX_DH
CONV_CH = D_MODEL // 2
GLA_HEADS = 4
GLA_DK = D_MODEL // 16
GLA_DV = D_MODEL // 8
GLA_RANK = 16
GLA_TAU = 16.0
GLA_CHUNK = 128
D_FF = D_MODEL * 11 // 4

COL_GA = 0
COL_GB = COL_GA + D_MODEL
COL_GC = COL_GB + D_MODEL
COL_QA = COL_GC + D_MODEL
COL_KA = COL_QA + FOX_WIDTH
COL_VA = COL_KA + FOX_WIDTH
COL_SCB = COL_VA + FOX_WIDTH
COL_SCC = COL_SCB + CONV_CH
COL_SCH = COL_SCC + CONV_CH
COL_QC = COL_SCH + CONV_CH
COL_KC = COL_QC + GLA_HEADS * GLA_DK
COL_VC = COL_KC + GLA_HEADS * GLA_DK
COL_RC = COL_VC + GLA_HEADS * GLA_DV
N_MAIN = COL_RC + GLA_HEADS * GLA_DV
N_SMALL = 128
SMALL_GLR = FOX_HEADS

SRC_FA = 3 * FOX_WIDTH
SRC_MID = SRC_FA + FOX_HEADS
SRC_GLR = SRC_MID + 3 * CONV_CH + 2 * GLA_HEADS * GLA_DK + 2 * GLA_HEADS * GLA_DV
SRC_GATE = SRC_GLR + GLA_RANK

ROW_TILE = 640
HALO = 16
VMEM_LIMIT = 56 * 1024 * 1024


def _row_tile(lp, cap, granule=128):
    return max(t for t in range(granule, min(cap, lp) + 1, granule) if lp % t == 0)


def _params(sem, vmem=VMEM_LIMIT):
    return pltpu.CompilerParams(dimension_semantics=sem, vmem_limit_bytes=vmem)


def _sigmoid(x):
    return 1.0 / (1.0 + jnp.exp(-x))


def _log_sigmoid(x):
    return jnp.minimum(x, 0.0) - jnp.log(1.0 + jnp.exp(-jnp.abs(x)))


def _dot(a, b):
    return jnp.dot(a, b, preferred_element_type=F32)


def _dot_nt(a, b):
    return lax.dot_general(a, b, (((1,), (1,)), ((), ())), preferred_element_type=F32)


def _split3(x):
    hi = x.astype(BF16)
    r1 = x - hi.astype(F32)
    mid = r1.astype(BF16)
    lo = (r1 - mid.astype(F32)).astype(BF16)
    return hi, mid, lo


def _embed_kernel(x_ref, xp_ref, meta_ref, g_ref, h_ref, n_ref, *, l_real):
    i = pl.program_id(0)
    tm = h_ref.shape[0]
    top = jnp.where(i == 0, meta_ref[...], xp_ref[...])
    hb = jnp.concatenate([top, x_ref[:tm - N_META, :]], axis=0)
    row = i * tm + lax.broadcasted_iota(jnp.int32, (tm, 1), 0)
    hb = jnp.where(row < l_real, hb, 0.0)
    h_ref[...] = hb
    y = hb * lax.rsqrt(jnp.mean(hb * hb, axis=-1, keepdims=True) + EPS)
    n_ref[...] = (y * g_ref[...]).astype(n_ref.dtype)


def _embed(x2, meta, g, lp, tm):
    seq, d = x2.shape
    last_x_block = (seq - 1) // tm
    mb = tm // N_META
    return pl.pallas_call(
        functools.partial(_embed_kernel, l_real=N_META + seq),
        grid=(lp // tm,),
        in_specs=[pl.BlockSpec((tm, d), lambda i: (jnp.minimum(i, last_x_block), 0)),
                  pl.BlockSpec((N_META, d), lambda i: (jnp.maximum(i * mb - 1, 0), 0)),
                  pl.BlockSpec((N_META, d), lambda i: (0, 0)),
                  pl.BlockSpec((1, d), lambda i: (0, 0))],
        out_specs=[pl.BlockSpec((tm, d), lambda i: (i, 0)),
                   pl.BlockSpec((tm, d), lambda i: (i, 0))],
        out_shape=[jax.ShapeDtypeStruct((lp, d), F32),
                   jax.ShapeDtypeStruct((lp, d), BF16)],
        compiler_params=_params(("parallel",)),
        name="embed",
    )(x2, x2, meta, g.reshape(1, d))


def _fox_cumsum_kernel(f_ref, b_ref, c_ref):
    nblk = f_ref.shape[0]
    row = lax.broadcasted_iota(jnp.int32, (128, 128), 0)
    col = lax.broadcasted_iota(jnp.int32, (128, 128), 1)
    triu = jnp.where(row <= col, 1.0, 0.0).astype(BF16)

    def body(j, carry):
        logf = _log_sigmoid(f_ref[j] + b_ref[...]) * LOG2_E
        hi, mid, lo = _split3(logf)
        cs = _dot(hi, triu) + _dot(mid, triu) + _dot(lo, triu) + carry
        c_ref[j] = cs
        return cs[:, 127:128]

    lax.fori_loop(0, nblk, body, jnp.zeros((f_ref.shape[1], 1), F32))


def _fox_cumsum(f_blocks, b_f):
    nblk, nh, _ = f_blocks.shape
    return pl.pallas_call(
        _fox_cumsum_kernel,
        out_shape=jax.ShapeDtypeStruct((nblk, nh, 128), F32),
        name="fox_cumsum",
    )(f_blocks, b_f.reshape(nh, 1))


def _lane_tile(x, n):
    return jnp.concatenate([x] * n, axis=1)


def _fox_kernel(q_ref, k_ref, v_ref, c_ref, o_ref, vx_sc, s_sc, p_sc, a_sc, m_sc, acc_sc,
                *, blk, part, sub, hg):
    i = pl.program_id(1)
    dh = q_ref.shape[1] // hg
    heads = range(hg)

    @pl.when(i == 0)
    def _():
        for h in heads:
            vx_sc[h, :, :dh] = v_ref[:, h * dh:(h + 1) * dh]
            vx_sc[h, :, dh:] = jnp.ones((vx_sc.shape[1], dh), BF16)

    m_sc[...] = jnp.full(m_sc.shape, NEG, F32)
    acc_sc[...] = jnp.zeros(acc_sc.shape, F32)

    def rows_of(j):
        return pl.ds(pl.multiple_of(j * blk, blk), blk)

    def logits(j, slot):
        for h in heads:
            kj = k_ref[rows_of(j), h * dh:(h + 1) * dh]
            for a in range(0, blk, part):
                s_sc[slot, h, a:a + part, :] = _dot_nt(q_ref[a:a + part, h * dh:(h + 1) * dh], kj)

    def softmax(j, slot, masked):
        def biased(h, r):
            s = s_sc[slot, h, r:r + sub, :] - c_ref[h, j]
            if masked:
                qpos = r + lax.broadcasted_iota(jnp.int32, (sub, blk), 0)
                kpos = lax.broadcasted_iota(jnp.int32, (sub, blk), 1)
                s = jnp.where(kpos <= qpos, s, NEG)
            return s

        for h in heads:
            for r in range(0, blk, sub):
                m_prev = m_sc[h, r:r + sub, :]
                m_new = jnp.maximum(m_prev, jnp.max(biased(h, r), axis=-1, keepdims=True))
                a_sc[slot, h, r:r + sub, :] = jnp.exp2(m_prev - m_new)
                m_sc[h, r:r + sub, :] = m_new
        for h in heads:
            for r in range(0, blk, sub):
                m_new = _lane_tile(m_sc[h, r:r + sub, :], blk // 128)
                p_sc[slot, h, r:r + sub, :] = jnp.exp2(biased(h, r) - m_new).astype(BF16)

    def values(j, slot):
        for h in heads:
            vj = vx_sc[h, rows_of(j), :]
            for a in range(0, blk, part):
                acc_sc[h, a:a + part, :] = (
                    acc_sc[h, a:a + part, :] * _lane_tile(a_sc[slot, h, a:a + part, :], 2)
                    + _dot(p_sc[slot, h, a:a + part, :], vj))

    def step(j, slot, masked, prefetch, prev):
        if prefetch:
            logits(j + 1, 1 - slot)
        if prev:
            values(j - 1, 1 - slot)
        softmax(j, slot, masked)

    def step_pair(t, _):
        step(2 * t + 1, 1, masked=False, prefetch=True, prev=True)
        step(2 * t + 2, 0, masked=False, prefetch=True, prev=True)
        return 0

    logits(0, 0)

    @pl.when(i == 0)
    def _():
        softmax(0, 0, masked=True)
        values(0, 0)

    @pl.when(i > 0)
    def _():
        step(0, 0, masked=False, prefetch=True, prev=False)
        lax.fori_loop(0, (i - 1) // 2, step_pair, 0)

        @pl.when(i % 2 == 0)
        def _():
            step(i - 1, 1, masked=False, prefetch=True, prev=True)
            step(i, 0, masked=True, prefetch=False, prev=True)
            values(i, 0)

        @pl.when(i % 2 == 1)
        def _():
            step(i, 1, masked=True, prefetch=False, prev=True)
            values(i, 1)

    for h in heads:
        o_ref[:, h * dh:(h + 1) * dh] = (acc_sc[h, :, :dh] / acc_sc[h, :, dh:]).astype(o_ref.dtype)


def _fox_attention(proj, c_blocks, blk, part, sub, hg):
    lp = proj.shape[0]
    nb = lp // blk
    dh = FOX_DH
    w = hg * dh
    kern = functools.partial(_fox_kernel, blk=blk, part=part, sub=sub, hg=hg)
    return pl.pallas_call(
        kern,
        grid=(FOX_HEADS // hg, nb),
        in_specs=[pl.BlockSpec((blk, w), lambda g, i: (i, COL_QA // w + g)),
                  pl.BlockSpec((lp, w), lambda g, i: (0, COL_KA // w + g)),
                  pl.BlockSpec((lp, w), lambda g, i: (0, COL_VA // w + g)),
                  pl.BlockSpec((hg, nb, 1, blk), lambda g, i: (g, 0, 0, 0))],
        out_specs=pl.BlockSpec((blk, w), lambda g, i: (i, g)),
        out_shape=jax.ShapeDtypeStruct((lp, FOX_WIDTH), BF16),
        scratch_shapes=[pltpu.VMEM((hg, lp, 2 * dh), BF16),
                        pltpu.VMEM((2, hg, blk, blk), F32),
                        pltpu.VMEM((2, hg, blk, blk), BF16),
                        pltpu.VMEM((2, hg, blk, 128), F32),
                        pltpu.VMEM((hg, blk, 128), F32),
                        pltpu.VMEM((hg, blk, 2 * dh), F32)],
        compiler_params=_params(("arbitrary", "arbitrary")),
        name="fox_attention",
    )(proj, proj, proj, c_blocks)


def _segment_reference_rows(b, n, sub8):
    c = b.shape[0]
    half = n // 2
    if n >= 16:
        pieces = [jnp.broadcast_to(b[m * n + half - 1:m * n + half, :], (n, b.shape[1]))
                  for m in range(c // n)]
        return pieces[0] if len(pieces) == 1 else jnp.concatenate(pieces, axis=0)
    b3 = b.reshape(c // 8, 8, b.shape[1])

    def bc(r):
        return jnp.broadcast_to(b3[:, r:r + 1, :], b3.shape).reshape(b.shape)

    out = bc(8 - n + half - 1)
    for start in range(8 - 2 * n, -1, -n):
        out = jnp.where(sub8 < start + n, bc(start + half - 1), out)
    return out


def _gla_kernel(q_ref, k_ref, v_ref, r_ref, sm_ref, wg_ref, bg_ref, ng_ref, o_ref, st_ref):
    c, dk, dv = GLA_CHUNK, GLA_DK, GLA_DV

    @pl.when(pl.program_id(0) == 0)
    def _():
        st_ref[...] = jnp.zeros(st_ref.shape, F32)

    t_idx = lax.broadcasted_iota(jnp.int32, (c, c), 0)
    s_idx = lax.broadcasted_iota(jnp.int32, (c, c), 1)
    tril = jnp.where(s_idx <= t_idx, 1.0, 0.0).astype(BF16)
    sub8 = t_idx & 7

    for ck, h in [(ck, h) for ck in range(q_ref.shape[0] // c) for h in range(GLA_HEADS)]:
        rows = slice(ck * c, (ck + 1) * c)
        q = q_ref[rows, h * dk:(h + 1) * dk].astype(F32)
        k = k_ref[rows, h * dk:(h + 1) * dk].astype(F32)
        v = v_ref[rows, h * dv:(h + 1) * dv]
        z = _dot(sm_ref[rows, :].astype(BF16), wg_ref[h].astype(BF16)) + bg_ref[h]
        g = _log_sigmoid(z) * (1.0 / GLA_TAU)
        hi, mid, lo = _split3(g)
        b = _dot(tril, hi) + _dot(tril, mid) + _dot(tril, lo)

        att = jnp.where(t_idx == s_idx, _dot_nt(q.astype(BF16), k.astype(BF16)), 0.0)
        n = c
        while n >= 2:
            upper = (t_idx & (n - 1)) >= n // 2
            ref = _segment_reference_rows(b, n, sub8)
            qs = jnp.where(upper, q * jnp.exp(jnp.minimum(b - ref, 0.0)), 0.0).astype(BF16)
            ks = jnp.where(upper, 0.0, k * jnp.exp(jnp.minimum(ref - b, 0.0))).astype(BF16)
            a = _dot_nt(qs, ks)
            if n < c:
                shift = n.bit_length() - 1
                a = jnp.where((t_idx >> shift) == (s_idx >> shift), a, 0.0)
            att = att + a
            n //= 2

        st = st_ref[h]
        o = _dot(att.astype(BF16), v) + _dot_nt((q * jnp.exp(b)).astype(BF16), st.astype(BF16))
        b_last = b[c - 1:c, :]
        ke = (k * jnp.exp(b_last - b)).astype(BF16)
        st_ref[h] = st * jnp.exp(b_last) + lax.dot_general(
            v, ke, (((0,), (0,)), ((), ())), preferred_element_type=F32)

        on = o * lax.rsqrt(jnp.mean(o * o, axis=-1, keepdims=True) + EPS) * ng_ref[h]
        r = r_ref[rows, h * dv:(h + 1) * dv].astype(F32)
        o_ref[rows, h * dv:(h + 1) * dv] = (r * _sigmoid(r) * on).astype(o_ref.dtype)


def _gla(proj, small, wg_pad, b_g, norm_g, rows):
    lp = proj.shape[0]
    c, dk, dv, nh = GLA_CHUNK, GLA_DK, GLA_DV, GLA_HEADS
    assert dk == c and rows % c == 0
    return pl.pallas_call(
        _gla_kernel,
        grid=(lp // rows,),
        in_specs=[pl.BlockSpec((rows, nh * dk), lambda i: (i, COL_QC // (nh * dk))),
                  pl.BlockSpec((rows, nh * dk), lambda i: (i, COL_KC // (nh * dk))),
                  pl.BlockSpec((rows, nh * dv), lambda i: (i, COL_VC // (nh * dv))),
                  pl.BlockSpec((rows, nh * dv), lambda i: (i, COL_RC // (nh * dv))),
                  pl.BlockSpec((rows, N_SMALL), lambda i: (i, 0)),
                  pl.BlockSpec((nh, N_SMALL, dk), lambda i: (0, 0, 0)),
                  pl.BlockSpec((nh, 1, dk), lambda i: (0, 0, 0)),
                  pl.BlockSpec((nh, 1, dv), lambda i: (0, 0, 0))],
        out_specs=pl.BlockSpec((rows, nh * dv), lambda i: (i, 0)),
        out_shape=jax.ShapeDtypeStruct((lp, nh * dv), BF16),
        scratch_shapes=[pltpu.VMEM((nh, dv, dk), F32)],
        compiler_params=_params(("arbitrary",)),
        name="gla",
    )(proj, proj, proj, proj, small, wg_pad, b_g, norm_g)


def _causal_conv3(ext, w_ref):
    m1 = pltpu.roll(ext, 1, 0)
    m2 = pltpu.roll(ext, 2, 0)
    return (ext[HALO:] * w_ref[2:3, :] + m1[HALO:] * w_ref[1:2, :] + m2[HALO:] * w_ref[0:1, :])


def _merge_kernel(oa_ref, scb_ref, scc_ref, sch_ref, scc_h_ref, sch_h_ref, oc_ref,
                  ga_ref, gb_ref, gc_ref, gbias_ref, cw_ref, wa_ref, wb_ref, wc_ref, o_ref):
    keep = (pl.program_id(0) > 0).astype(F32)
    p_halo = scc_h_ref[...].astype(F32) * sch_h_ref[...].astype(F32) * keep
    p = scc_ref[...].astype(F32) * sch_ref[...].astype(F32)
    conv = _causal_conv3(jnp.concatenate([p_halo, p], axis=0), cw_ref)
    ub = (scb_ref[...].astype(F32) * conv).astype(BF16)
    ya = _dot(oa_ref[...], wa_ref[...])
    yb = _dot(ub, wb_ref[...])
    yc = _dot(oc_ref[...], wc_ref[...])
    mix = (_sigmoid(ga_ref[...].astype(F32) + gbias_ref[0:1, :]) * ya
           + _sigmoid(gb_ref[...].astype(F32) + gbias_ref[1:2, :]) * yb
           + _sigmoid(gc_ref[...].astype(F32) + gbias_ref[2:3, :]) * yc)
    o_ref[...] = mix.astype(o_ref.dtype)


def _merge(proj, oa, oc, gate_b, conv_w, w_a_o, w_b_o, w_c_o, layer, tm):
    lp = proj.shape[0]
    d = D_MODEL
    ch = CONV_CH
    hb = tm // HALO

    def col(c0, width):
        return lambda i: (i, c0 // width)

    def halo(c0):
        return lambda i: (jnp.maximum(i * hb - 1, 0), c0 // ch)

    def const(shape):
        return pl.BlockSpec(shape, lambda i: (0, 0), pipeline_mode=pl.Buffered(1))

    def weight(k):
        return pl.BlockSpec((None, k, d), lambda i: (layer, 0, 0), pipeline_mode=pl.Buffered(1))

    return pl.pallas_call(
        _merge_kernel,
        grid=(lp // tm,),
        in_specs=[pl.BlockSpec((tm, FOX_WIDTH), lambda i: (i, 0)),
                  pl.BlockSpec((tm, ch), col(COL_SCB, ch)),
                  pl.BlockSpec((tm, ch), col(COL_SCC, ch)),
                  pl.BlockSpec((tm, ch), col(COL_SCH, ch)),
                  pl.BlockSpec((HALO, ch), halo(COL_SCC)),
                  pl.BlockSpec((HALO, ch), halo(COL_SCH)),
                  pl.BlockSpec((tm, GLA_HEADS * GLA_DV), lambda i: (i, 0)),
                  pl.BlockSpec((tm, d), col(COL_GA, d)),
                  pl.BlockSpec((tm, d), col(COL_GB, d)),
                  pl.BlockSpec((tm, d), col(COL_GC, d)),
                  const((3, d)),
                  const((3, ch)),
                  weight(FOX_WIDTH),
                  weight(ch),
                  weight(GLA_HEADS * GLA_DV)],
        out_specs=pl.BlockSpec((tm, d), lambda i: (i, 0)),
        out_shape=jax.ShapeDtypeStruct((lp, d), BF16),
        compiler_params=_params(("parallel",)),
        name="merge",
    )(oa, proj, proj, proj, proj, proj, oc, proj, proj, proj,
      gate_b.reshape(3, d), conv_w, w_a_o, w_b_o, w_c_o)


def _resid_matmul_kernel(x_ref, w_ref, h_ref, g_ref, ho_ref, no_ref):
    hn = h_ref[...] + _dot(x_ref[...], w_ref[...])
    ho_ref[...] = hn
    y = hn * lax.rsqrt(jnp.mean(hn * hn, axis=-1, keepdims=True) + EPS)
    no_ref[...] = (y * g_ref[...]).astype(no_ref.dtype)


def _resid_matmul(x, w, layer, h, g, norm_dtype, tm, name):
    lp, k = x.shape
    d = w.shape[2]
    return pl.pallas_call(
        _resid_matmul_kernel,
        grid=(lp // tm,),
        in_specs=[pl.BlockSpec((tm, k), lambda i: (i, 0)),
                  pl.BlockSpec((None, k, d), lambda i: (layer, 0, 0), pipeline_mode=pl.Buffered(1)),
                  pl.BlockSpec((tm, d), lambda i: (i, 0)),
                  pl.BlockSpec((1, d), lambda i: (0, 0))],
        out_specs=[pl.BlockSpec((tm, d), lambda i: (i, 0)),
                   pl.BlockSpec((tm, d), lambda i: (i, 0))],
        out_shape=[jax.ShapeDtypeStruct((lp, d), F32),
                   jax.ShapeDtypeStruct((lp, d), norm_dtype)],
        compiler_params=_params(("parallel",)),
        name=name,
    )(x, w, h, g.reshape(1, d))


def _matmul_resid_kernel(x_ref, w_ref, h_ref, o_ref):
    o_ref[...] = h_ref[...] + _dot(x_ref[...], w_ref[...])


def _matmul_resid(x, w, layer, h, tm, tn, name):
    lp, k = x.shape
    n = w.shape[2]
    return pl.pallas_call(
        _matmul_resid_kernel,
        grid=(lp // tm, n // tn),
        in_specs=[pl.BlockSpec((tm, k), lambda i, j: (i, 0)),
                  pl.BlockSpec((None, k, tn), lambda i, j: (layer, 0, j)),
                  pl.BlockSpec((tm, tn), lambda i, j: (i, j))],
        out_specs=pl.BlockSpec((tm, tn), lambda i, j: (i, j)),
        out_shape=jax.ShapeDtypeStruct((lp, n), F32),
        compiler_params=_params(("parallel", "arbitrary")),
        name=name,
    )(x, w, h)


def _rmsnorm_kernel(h_ref, g_ref, o_ref):
    x = h_ref[...]
    y = x * lax.rsqrt(jnp.mean(x * x, axis=-1, keepdims=True) + EPS)
    o_ref[...] = (y * g_ref[...]).astype(o_ref.dtype)


def _rmsnorm(h, g, tm):
    lp, d = h.shape
    return pl.pallas_call(
        _rmsnorm_kernel,
        grid=(lp // tm,),
        in_specs=[pl.BlockSpec((tm, d), lambda i: (i, 0)),
                  pl.BlockSpec((1, d), lambda i: (0, 0))],
        out_specs=pl.BlockSpec((tm, d), lambda i: (i, 0)),
        out_shape=jax.ShapeDtypeStruct((lp, d), BF16),
        compiler_params=_params(("parallel",)),
        name="rmsnorm",
    )(h, g.reshape(1, d))


def _final_norm_kernel(h_ref, hn_ref, g_ref, o_ref):
    x = jnp.concatenate([h_ref[N_META:, :], hn_ref[...]], axis=0)
    y = x * lax.rsqrt(jnp.mean(x * x, axis=-1, keepdims=True) + EPS)
    o_ref[...] = y * g_ref[...]


def _final_norm(h, g, seq, tm):
    lp, d = h.shape
    mb = tm // N_META
    last = lp // N_META - 1
    return pl.pallas_call(
        _final_norm_kernel,
        grid=(-(-seq // tm),),
        in_specs=[pl.BlockSpec((tm, d), lambda i: (i, 0)),
                  pl.BlockSpec((N_META, d), lambda i: (jnp.minimum((i + 1) * mb, last), 0)),
                  pl.BlockSpec((1, d), lambda i: (0, 0))],
        out_specs=pl.BlockSpec((tm, d), lambda i: (i, 0)),
        out_shape=jax.ShapeDtypeStruct((seq, d), F32),
        compiler_params=_params(("parallel",)),
        name="final_norm",
    )(h, h, g.reshape(1, d))


def _mlp_up_kernel(x_ref, xh_ref, wg_ref, wu_ref, cg_ref, cu_ref, o_ref, *, n_sub):
    rs = x_ref.shape[0] // n_sub
    keep = (pl.program_id(0) > 0).astype(BF16)
    wg = wg_ref[...].astype(BF16)
    wu = wu_ref[...].astype(BF16)
    for s in range(n_sub):
        if s == 0:
            xe = jnp.concatenate([xh_ref[...] * keep, x_ref[:rs, :]], axis=0)
        else:
            xe = x_ref[s * rs - HALO:(s + 1) * rs, :]
        ug = _causal_conv3(_dot(xe, wg), cg_ref)
        uu = _causal_conv3(_dot(xe, wu), cu_ref)
        o_ref[s * rs:(s + 1) * rs, :] = (ug * _sigmoid(ug) * uu).astype(o_ref.dtype)


def _mlp_up(xn, w_up, layer, conv_w, tm, tn):
    lp, d = xn.shape
    nj = D_FF // tn
    hb = tm // HALO
    n_sub = 4 if tm % (4 * HALO) == 0 else 1
    return pl.pallas_call(
        functools.partial(_mlp_up_kernel, n_sub=n_sub),
        grid=(lp // tm, nj),
        in_specs=[pl.BlockSpec((tm, d), lambda i, j: (i, 0)),
                  pl.BlockSpec((HALO, d), lambda i, j: (jnp.maximum(i * hb - 1, 0), 0)),
                  pl.BlockSpec((None, d, tn), lambda i, j: (layer, 0, j)),
                  pl.BlockSpec((None, d, tn), lambda i, j: (layer, 0, nj + j)),
                  pl.BlockSpec((3, tn), lambda i, j: (0, j)),
                  pl.BlockSpec((3, tn), lambda i, j: (0, nj + j))],
        out_specs=pl.BlockSpec((tm, tn), lambda i, j: (i, j)),
        out_shape=jax.ShapeDtypeStruct((lp, D_FF), BF16),
        compiler_params=_params(("parallel", "arbitrary")),
        name="mlp_up",
    )(xn, xn, w_up, w_up, conv_w, conv_w)


def _in_proj_kernel(x_ref, w_ref, ws_ref, s_ref, o_ref, os_ref):
    j = pl.program_id(1)
    last = pl.num_programs(1) - 1

    @pl.when(j < last)
    def _():
        w = w_ref[...].astype(BF16)
        o_ref[...] = (_dot_nt(x_ref[...], w) * s_ref[...]).astype(o_ref.dtype)

    @pl.when(j == last)
    def _():
        os_ref[...] = _dot_nt(x_ref[...], ws_ref[...].astype(BF16))


def _in_proj(xn, w_t, w_small_t, layer, colscale, tm, tn):
    lp, k = xn.shape
    nj = N_MAIN // tn
    assert COL_QA % tn == 0 and COL_SCB % tn == 0 and N_MAIN % tn == 0
    assert SRC_GATE % 8 == 0 and SRC_MID % 8 == 0 and tn % 8 == 0

    def main(j):
        return jnp.minimum(j, nj - 1)

    def src_row(j):
        c8 = main(j) * (tn // 8)
        r8 = jnp.where(c8 < COL_QA // 8, SRC_GATE // 8 + c8,
                       jnp.where(c8 < COL_SCB // 8, c8 - COL_QA // 8,
                                 (SRC_MID - COL_SCB) // 8 + c8))
        return r8 * 8

    return pl.pallas_call(
        _in_proj_kernel,
        grid=(lp // tm, nj + 1),
        in_specs=[pl.BlockSpec((tm, k), lambda i, j: (i, 0)),
                  pl.BlockSpec((None, pl.Element(tn), pl.Element(k)),
                               lambda i, j: (layer, src_row(j), 0)),
                  pl.BlockSpec((None, N_SMALL, k), lambda i, j: (layer, 0, 0)),
                  pl.BlockSpec((1, tn), lambda i, j: (0, main(j)))],
        out_specs=[pl.BlockSpec((tm, tn), lambda i, j: (i, main(j))),
                   pl.BlockSpec((tm, N_SMALL), lambda i, j: (i, 0))],
        out_shape=[jax.ShapeDtypeStruct((lp, N_MAIN), BF16),
                   jax.ShapeDtypeStruct((lp, N_SMALL), F32)],
        compiler_params=_params(("parallel", "arbitrary")),
        name="in_proj",
    )(xn, w_t, w_small_t, colscale)


def _small_w_in(w_t):
    pad = jnp.zeros((w_t.shape[0], N_SMALL - FOX_HEADS - GLA_RANK, w_t.shape[2]), w_t.dtype)
    return jnp.concatenate([w_t[:, SRC_FA:SRC_MID], w_t[:, SRC_GLR:SRC_GATE], pad], axis=1)


def _main_colscale():
    s = jnp.ones((1, N_MAIN), F32)
    s = s.at[:, COL_QA:COL_QA + FOX_WIDTH].set(FOX_DH ** -0.5 * LOG2_E)
    s = s.at[:, COL_QC:COL_QC + GLA_HEADS * GLA_DK].set(GLA_DK ** -0.5)
    return s


def kernel(x, meta_tokens, norm1_g, w_in, fox_b_f, gate_b, conv_w, gla_w_g2, gla_b_g, gla_norm_g,
           w_a_o, w_b_o, w_c_o, w_o, norm2_g, w_up, mlp_conv_w, w_down, final_norm_g):
    assert x.shape[0] == 1 and x.shape[2] == D_MODEL
    depth = w_in.shape[0]
    seq = x.shape[1]
    l_real = N_META + seq
    lp = -(-l_real // ROW_TILE) * ROW_TILE
    tm = ROW_TILE
    blk = ROW_TILE

    h, xn = _embed(x[0], meta_tokens.astype(x.dtype), norm1_g[0], lp, tm)
    colscale = _main_colscale()
    w_t = jnp.swapaxes(w_in, 1, 2)
    w_small_t = _small_w_in(w_t)
    w_a_o, w_b_o, w_c_o, w_o, w_down = (w.astype(BF16) for w in (w_a_o, w_b_o, w_c_o, w_o, w_down))

    for l in range(depth):
        proj, small = _in_proj(xn, w_t, w_small_t, l, colscale, _row_tile(lp, 1664), 1024)

        f_blocks = small[:, :FOX_HEADS].T.reshape(FOX_HEADS, lp // 128, 128).transpose(1, 0, 2)
        c = _fox_cumsum(f_blocks, fox_b_f[l])
        c_blocks = c.transpose(1, 0, 2).reshape(FOX_HEADS, lp // blk, 1, blk)
        oa = _fox_attention(proj, c_blocks, blk, 320, 64, 2)

        wg = gla_w_g2[l].reshape(GLA_RANK, GLA_HEADS, GLA_DK).transpose(1, 0, 2)
        wg_pad = jnp.zeros((GLA_HEADS, N_SMALL, GLA_DK), F32)
        wg_pad = wg_pad.at[:, SMALL_GLR:SMALL_GLR + GLA_RANK, :].set(wg)
        oc = _gla(proj, small, wg_pad, gla_b_g[l].reshape(GLA_HEADS, 1, GLA_DK),
                  gla_norm_g[l].reshape(GLA_HEADS, 1, GLA_DV), ROW_TILE)

        mix = _merge(proj, oa, oc, gate_b[l], conv_w[l], w_a_o, w_b_o, w_c_o, l, 320)
        h, xn2 = _resid_matmul(mix, w_o, l, h, norm2_g[l], BF16, 640, "out_proj")

        act = _mlp_up(xn2, w_up, l, mlp_conv_w[l], _row_tile(lp, 1664), 512)
        h = _matmul_resid(act, w_down, l, h, _row_tile(lp, 832, 64), 512, "mlp_down")
        if l + 1 < depth:
            xn = _rmsnorm(h, norm1_g[l + 1], tm)
    return _final_norm(h, final_norm_g, seq, tm)[None]
```

```python
import functools

import jax
import jax.numpy as jnp
from jax import lax
from jax.experimental import pallas as pl
from jax.experimental.pallas import tpu as pltpu

F32 = jnp.float32
BF16 = jnp.bfloat16

D_MODEL = 2048
N_META = 16
EPS = 1e-6
NEG = -1e30
LOG2_E = 1.4426950408889634

FOX_HEADS = 8
FOX_DH = D_MODEL // 16
FOX_WIDTH = FOX_HEADS * FOX_DH
CONV_CH = D_MODEL // 2
GLA_HEADS = 4
GLA_DK = D_MODEL // 16
GLA_DV = D_MODEL // 8
GLA_RANK = 16
GLA_TAU = 16.0
GLA_CHUNK = 128
D_FF = D_MODEL * 11 // 4

COL_GA = 0
COL_GB = COL_GA + D_MODEL
COL_GC = COL_GB + D_MODEL
COL_QA = COL_GC + D_MODEL
COL_KA = COL_QA + FOX_WIDTH
COL_VA = COL_KA + FOX_WIDTH
COL_SCB = COL_VA + FOX_WIDTH
COL_SCC = COL_SCB + CONV_CH
COL_SCH = COL_SCC + CONV_CH
COL_QC = COL_SCH + CONV_CH
COL_KC = COL_QC + GLA_HEADS * GLA_DK
COL_VC = COL_KC + GLA_HEADS * GLA_DK
COL_RC = COL_VC + GLA_HEADS * GLA_DV
N_MAIN = COL_RC + GLA_HEADS * GLA_DV
N_SMALL = 128
SMALL_GLR = FOX_HEADS

SRC_FA = 3 * FOX_WIDTH
SRC_MID = SRC_FA + FOX_HEADS
SRC_GLR = SRC_MID + 3 * CONV_CH + 2 * GLA_HEADS * GLA_DK + 2 * GLA_HEADS * GLA_DV
SRC_GATE = SRC_GLR + GLA_RANK

ROW_TILE = 640
HALO = 16
VMEM_LIMIT = 56 * 1024 * 1024


def _row_tile(lp, cap, granule=128):
    return max(t for t in range(granule, min(cap, lp) + 1, granule) if lp % t == 0)


def _params(sem, vmem=VMEM_LIMIT):
    return pltpu.CompilerParams(dimension_semantics=sem, vmem_limit_bytes=vmem)


def _sigmoid(x):
    return 1.0 / (1.0 + jnp.exp(-x))


def _log_sigmoid(x):
    return jnp.minimum(x, 0.0) - jnp.log(1.0 + jnp.exp(-jnp.abs(x)))


def _dot(a, b):
    return jnp.dot(a, b, preferred_element_type=F32)


def _dot_nt(a, b):
    return lax.dot_general(a, b, (((1,), (1,)), ((), ())), preferred_element_type=F32)


def _split3(x):
    hi = x.astype(BF16)
    r1 = x - hi.astype(F32)
    mid = r1.astype(BF16)
    lo = (r1 - mid.astype(F32)).astype(BF16)
    return hi, mid, lo


def _embed_kernel(x_ref, xp_ref, meta_ref, g_ref, h_ref, n_ref, *, l_real):
    i = pl.program_id(0)
    tm = h_ref.shape[0]
    top = jnp.where(i == 0, meta_ref[...], xp_ref[...])
    hb = jnp.concatenate([top, x_ref[:tm - N_META, :]], axis=0)
    row = i * tm + lax.broadcasted_iota(jnp.int32, (tm, 1), 0)
    hb = jnp.where(row < l_real, hb, 0.0)
    h_ref[...] = hb
    y = hb * lax.rsqrt(jnp.mean(hb * hb, axis=-1, keepdims=True) + EPS)
    n_ref[...] = (y * g_ref[...]).astype(n_ref.dtype)


def _embed(x2, meta, g, lp, tm):
    seq, d = x2.shape
    last_x_block = (seq - 1) // tm
    mb = tm // N_META
    return pl.pallas_call(
        functools.partial(_embed_kernel, l_real=N_META + seq),
        grid=(lp // tm,),
        in_specs=[pl.BlockSpec((tm, d), lambda i: (jnp.minimum(i, last_x_block), 0)),
                  pl.BlockSpec((N_META, d), lambda i: (jnp.maximum(i * mb - 1, 0), 0)),
                  pl.BlockSpec((N_META, d), lambda i: (0, 0)),
                  pl.BlockSpec((1, d), lambda i: (0, 0))],
        out_specs=[pl.BlockSpec((tm, d), lambda i: (i, 0)),
                   pl.BlockSpec((tm, d), lambda i: (i, 0))],
        out_shape=[jax.ShapeDtypeStruct((lp, d), F32),
                   jax.ShapeDtypeStruct((lp, d), BF16)],
        compiler_params=_params(("parallel",)),
        name="embed",
    )(x2, x2, meta, g.reshape(1, d))


def _fox_cumsum_kernel(f_ref, b_ref, c_ref):
    nblk = f_ref.shape[0]
    row = lax.broadcasted_iota(jnp.int32, (128, 128), 0)
    col = lax.broadcasted_iota(jnp.int32, (128, 128), 1)
    triu = jnp.where(row <= col, 1.0, 0.0).astype(BF16)

    carry = jnp.zeros((f_ref.shape[1], 1), F32)
    for j in range(nblk):
        logf = _log_sigmoid(f_ref[j] + b_ref[...]) * LOG2_E
        hi, mid, lo = _split3(logf)
        local = _dot(hi, triu) + _dot(mid, triu) + _dot(lo, triu)
        c_ref[j] = local + carry
        carry = carry + local[:, 127:128]


def _fox_cumsum(f_blocks, b_f):
    nblk, nh, _ = f_blocks.shape
    return pl.pallas_call(
        _fox_cumsum_kernel,
        out_shape=jax.ShapeDtypeStruct((nblk, nh, 128), F32),
        name="fox_cumsum",
    )(f_blocks, b_f.reshape(nh, 1))


def _lane_tile(x, n):
    return jnp.concatenate([x] * n, axis=1)


def _fox_kernel(q_ref, k_ref, v_ref, c_ref, o_ref, vx_sc, s_sc, p_sc, a_sc, m_sc, acc_sc,
                *, blk, part, sub, hg):
    i = pl.program_id(1)
    dh = q_ref.shape[1] // hg
    heads = range(hg)

    @pl.when(i == 0)
    def _():
        for h in heads:
            vx_sc[h, :, :dh] = v_ref[:, h * dh:(h + 1) * dh]
            vx_sc[h, :, dh:] = jnp.ones((vx_sc.shape[1], dh), BF16)

    m_sc[...] = jnp.full(m_sc.shape, NEG, F32)
    acc_sc[...] = jnp.zeros(acc_sc.shape, F32)

    def rows_of(j):
        return pl.ds(pl.multiple_of(j * blk, blk), blk)

    def logits(j, slot):
        for h in heads:
            kj = k_ref[rows_of(j), h * dh:(h + 1) * dh]
            for a in range(0, blk, part):
                s_sc[slot, h, a:a + part, :] = _dot_nt(q_ref[a:a + part, h * dh:(h + 1) * dh], kj)

    def softmax(j, slot, masked):
        def biased(h, r):
            s = s_sc[slot, h, r:r + sub, :] - c_ref[h, j]
            if masked:
                qpos = r + lax.broadcasted_iota(jnp.int32, (sub, blk), 0)
                kpos = lax.broadcasted_iota(jnp.int32, (sub, blk), 1)
                s = jnp.where(kpos <= qpos, s, NEG)
            return s

        for h in heads:
            for r in range(0, blk, sub):
                m_prev = m_sc[h, r:r + sub, :]
                m_new = jnp.maximum(m_prev, jnp.max(biased(h, r), axis=-1, keepdims=True))
                a_sc[slot, h, r:r + sub, :] = jnp.exp2(m_prev - m_new)
                m_sc[h, r:r + sub, :] = m_new
        for h in heads:
            for r in range(0, blk, sub):
                m_new = _lane_tile(m_sc[h, r:r + sub, :], blk // 128)
                p_sc[slot, h, r:r + sub, :] = jnp.exp2(biased(h, r) - m_new).astype(BF16)

    def values(j, slot):
        for h in heads:
            vj = vx_sc[h, rows_of(j), :]
            for a in range(0, blk, part):
                acc_sc[h, a:a + part, :] = (
                    acc_sc[h, a:a + part, :] * _lane_tile(a_sc[slot, h, a:a + part, :], 2)
                    + _dot(p_sc[slot, h, a:a + part, :], vj))

    def step(j, slot, masked, prefetch, prev):
        if prefetch:
            logits(j + 1, 1 - slot)
        if prev:
            values(j - 1, 1 - slot)
        softmax(j, slot, masked)

    def step_pair(t, _):
        step(2 * t + 1, 1, masked=False, prefetch=True, prev=True)
        step(2 * t + 2, 0, masked=False, prefetch=True, prev=True)
        return 0

    logits(0, 0)

    @pl.when(i == 0)
    def _():
        softmax(0, 0, masked=True)
        values(0, 0)

    @pl.when(i > 0)
    def _():
        step(0, 0, masked=False, prefetch=True, prev=False)
        lax.fori_loop(0, (i - 1) // 2, step_pair, 0)

        @pl.when(i % 2 == 0)
        def _():
            step(i - 1, 1, masked=False, prefetch=True, prev=True)
            step(i, 0, masked=True, prefetch=False, prev=True)
            values(i, 0)

        @pl.when(i % 2 == 1)
        def _():
            step(i, 1, masked=True, prefetch=False, prev=True)
            values(i, 1)

    for h in heads:
        o_ref[:, h * dh:(h + 1) * dh] = (acc_sc[h, :, :dh] / acc_sc[h, :, dh:]).astype(o_ref.dtype)


def _fox_attention(proj, c_blocks, blk, part, sub, hg):
    lp = proj.shape[0]
    nb = lp // blk
    dh = FOX_DH
    w = hg * dh
    kern = functools.partial(_fox_kernel, blk=blk, part=part, sub=sub, hg=hg)
    return pl.pallas_call(
        kern,
        grid=(FOX_HEADS // hg, nb),
        in_specs=[pl.BlockSpec((blk, w), lambda g, i: (i, COL_QA // w + g)),
                  pl.BlockSpec((lp, w), lambda g, i: (0, COL_KA // w + g)),
                  pl.BlockSpec((lp, w), lambda g, i: (0, COL_VA // w + g)),
                  pl.BlockSpec((hg, nb, 1, blk), lambda g, i: (g, 0, 0, 0))],
        out_specs=pl.BlockSpec((blk, w), lambda g, i: (i, g)),
        out_shape=jax.ShapeDtypeStruct((lp, FOX_WIDTH), BF16),
        scratch_shapes=[pltpu.VMEM((hg, lp, 2 * dh), BF16),
                        pltpu.VMEM((2, hg, blk, blk), F32),
                        pltpu.VMEM((2, hg, blk, blk), BF16),
                        pltpu.VMEM((2, hg, blk, 128), F32),
                        pltpu.VMEM((hg, blk, 128), F32),
                        pltpu.VMEM((hg, blk, 2 * dh), F32)],
        compiler_params=_params(("arbitrary", "arbitrary")),
        name="fox_attention",
    )(proj, proj, proj, c_blocks)


def _segment_reference_rows(b, n, sub8):
    c = b.shape[0]
    half = n // 2
    if n >= 16:
        pieces = [jnp.broadcast_to(b[m * n + half - 1:m * n + half, :], (n, b.shape[1]))
                  for m in range(c // n)]
        return pieces[0] if len(pieces) == 1 else jnp.concatenate(pieces, axis=0)
    b3 = b.reshape(c // 8, 8, b.shape[1])

    def bc(r):
        return jnp.broadcast_to(b3[:, r:r + 1, :], b3.shape).reshape(b.shape)

    out = bc(8 - n + half - 1)
    for start in range(8 - 2 * n, -1, -n):
        out = jnp.where(sub8 < start + n, bc(start + half - 1), out)
    return out


def _gla_kernel(q_ref, k_ref, v_ref, r_ref, sm_ref, wg_ref, bg_ref, ng_ref, o_ref, st_ref):
    c, dk, dv = GLA_CHUNK, GLA_DK, GLA_DV

    @pl.when(pl.program_id(0) == 0)
    def _():
        st_ref[...] = jnp.zeros(st_ref.shape, F32)

    t_idx = lax.broadcasted_iota(jnp.int32, (c, c), 0)
    s_idx = lax.broadcasted_iota(jnp.int32, (c, c), 1)
    tril = jnp.where(s_idx <= t_idx, 1.0, 0.0).astype(BF16)
    sub8 = t_idx & 7

    for ck, h in [(ck, h) for ck in range(q_ref.shape[0] // c) for h in range(GLA_HEADS)]:
        rows = slice(ck * c, (ck + 1) * c)
        q = q_ref[rows, h * dk:(h + 1) * dk].astype(F32)
        k = k_ref[rows, h * dk:(h + 1) * dk].astype(F32)
        v = v_ref[rows, h * dv:(h + 1) * dv]
        z = _dot(sm_ref[rows, :].astype(BF16), wg_ref[h].astype(BF16)) + bg_ref[h]
        g = _log_sigmoid(z) * (1.0 / GLA_TAU)
        hi, mid, lo = _split3(g)
        b = _dot(tril, hi) + _dot(tril, mid) + _dot(tril, lo)

        att = jnp.where(t_idx == s_idx, _dot_nt(q.astype(BF16), k.astype(BF16)), 0.0)
        n = c
        while n >= 2:
            upper = (t_idx & (n - 1)) >= n // 2
            ref = _segment_reference_rows(b, n, sub8)
            qs = jnp.where(upper, q * jnp.exp(jnp.minimum(b - ref, 0.0)), 0.0).astype(BF16)
            ks = jnp.where(upper, 0.0, k * jnp.exp(jnp.minimum(ref - b, 0.0))).astype(BF16)
            a = _dot_nt(qs, ks)
            if n < c:
                shift = n.bit_length() - 1
                a = jnp.where((t_idx >> shift) == (s_idx >> shift), a, 0.0)
            att = att + a
            n //= 2

        st = st_ref[h]
        o = _dot(att.astype(BF16), v) + _dot_nt((q * jnp.exp(b)).astype(BF16), st.astype(BF16))
        b_last = b[c - 1:c, :]
        ke = (k * jnp.exp(b_last - b)).astype(BF16)
        st_ref[h] = st * jnp.exp(b_last) + lax.dot_general(
            v, ke, (((0,), (0,)), ((), ())), preferred_element_type=F32)

        on = o * lax.rsqrt(jnp.mean(o * o, axis=-1, keepdims=True) + EPS) * ng_ref[h]
        r = r_ref[rows, h * dv:(h + 1) * dv].astype(F32)
        o_ref[rows, h * dv:(h + 1) * dv] = (r * _sigmoid(r) * on).astype(o_ref.dtype)


def _gla(proj, small, wg_pad, b_g, norm_g, rows):
    lp = proj.shape[0]
    c, dk, dv, nh = GLA_CHUNK, GLA_DK, GLA_DV, GLA_HEADS
    assert dk == c and rows % c == 0
    return pl.pallas_call(
        _gla_kernel,
        grid=(lp // rows,),
        in_specs=[pl.BlockSpec((rows, nh * dk), lambda i: (i, COL_QC // (nh * dk))),
                  pl.BlockSpec((rows, nh * dk), lambda i: (i, COL_KC // (nh * dk))),
                  pl.BlockSpec((rows, nh * dv), lambda i: (i, COL_VC // (nh * dv))),
                  pl.BlockSpec((rows, nh * dv), lambda i: (i, COL_RC // (nh * dv))),
                  pl.BlockSpec((rows, N_SMALL), lambda i: (i, 0)),
                  pl.BlockSpec((nh, N_SMALL, dk), lambda i: (0, 0, 0)),
                  pl.BlockSpec((nh, 1, dk), lambda i: (0, 0, 0)),
                  pl.BlockSpec((nh, 1, dv), lambda i: (0, 0, 0))],
        out_specs=pl.BlockSpec((rows, nh * dv), lambda i: (i, 0)),
        out_shape=jax.ShapeDtypeStruct((lp, nh * dv), BF16),
        scratch_shapes=[pltpu.VMEM((nh, dv, dk), F32)],
        compiler_params=_params(("arbitrary",)),
        name="gla",
    )(proj, proj, proj, proj, small, wg_pad, b_g, norm_g)


def _causal_conv3(ext, w_ref):
    m1 = pltpu.roll(ext, 1, 0)
    m2 = pltpu.roll(ext, 2, 0)
    return (ext[HALO:] * w_ref[2:3, :] + m1[HALO:] * w_ref[1:2, :] + m2[HALO:] * w_ref[0:1, :])


def _merge_kernel(oa_ref, scb_ref, scc_ref, sch_ref, scc_h_ref, sch_h_ref, oc_ref,
                  ga_ref, gb_ref, gc_ref, gbias_ref, cw_ref, wa_ref, wb_ref, wc_ref, o_ref):
    keep = (pl.program_id(0) > 0).astype(F32)
    p_halo = scc_h_ref[...].astype(F32) * sch_h_ref[...].astype(F32) * keep
    p = scc_ref[...].astype(F32) * sch_ref[...].astype(F32)
    conv = _causal_conv3(jnp.concatenate([p_halo, p], axis=0), cw_ref)
    ub = (scb_ref[...].astype(F32) * conv).astype(BF16)
    ya = _dot(oa_ref[...], wa_ref[...])
    yb = _dot(ub, wb_ref[...])
    yc = _dot(oc_ref[...], wc_ref[...])
    mix = (_sigmoid(ga_ref[...].astype(F32) + gbias_ref[0:1, :]) * ya
           + _sigmoid(gb_ref[...].astype(F32) + gbias_ref[1:2, :]) * yb
           + _sigmoid(gc_ref[...].astype(F32) + gbias_ref[2:3, :]) * yc)
    o_ref[...] = mix.astype(o_ref.dtype)


def _merge(proj, oa, oc, gate_b, conv_w, w_a_o, w_b_o, w_c_o, layer, tm):
    lp = proj.shape[0]
    d = D_MODEL
    ch = CONV_CH
    hb = tm // HALO

    def col(c0, width):
        return lambda i: (i, c0 // width)

    def halo(c0):
        return lambda i: (jnp.maximum(i * hb - 1, 0), c0 // ch)

    def const(shape):
        return pl.BlockSpec(shape, lambda i: (0, 0), pipeline_mode=pl.Buffered(1))

    def weight(k):
        return pl.BlockSpec((None, k, d), lambda i: (layer, 0, 0), pipeline_mode=pl.Buffered(1))

    return pl.pallas_call(
        _merge_kernel,
        grid=(lp // tm,),
        in_specs=[pl.BlockSpec((tm, FOX_WIDTH), lambda i: (i, 0)),
                  pl.BlockSpec((tm, ch), col(COL_SCB, ch)),
                  pl.BlockSpec((tm, ch), col(COL_SCC, ch)),
                  pl.BlockSpec((tm, ch), col(COL_SCH, ch)),
                  pl.BlockSpec((HALO, ch), halo(COL_SCC)),
                  pl.BlockSpec((HALO, ch), halo(COL_SCH)),
                  pl.BlockSpec((tm, GLA_HEADS * GLA_DV), lambda i: (i, 0)),
                  pl.BlockSpec((tm, d), col(COL_GA, d)),
                  pl.BlockSpec((tm, d), col(COL_GB, d)),
                  pl.BlockSpec((tm, d), col(COL_GC, d)),
                  const((3, d)),
                  const((3, ch)),
                  weight(FOX_WIDTH),
                  weight(ch),
                  weight(GLA_HEADS * GLA_DV)],
        out_specs=pl.BlockSpec((tm, d), lambda i: (i, 0)),
        out_shape=jax.ShapeDtypeStruct((lp, d), BF16),
        compiler_params=_params(("parallel",)),
        name="merge",
    )(oa, proj, proj, proj, proj, proj, oc, proj, proj, proj,
      gate_b.reshape(3, d), conv_w, w_a_o, w_b_o, w_c_o)


def _resid_matmul_kernel(x_ref, w_ref, h_ref, g_ref, ho_ref, no_ref):
    hn = h_ref[...] + _dot(x_ref[...], w_ref[...])
    ho_ref[...] = hn
    y = hn * lax.rsqrt(jnp.mean(hn * hn, axis=-1, keepdims=True) + EPS)
    no_ref[...] = (y * g_ref[...]).astype(no_ref.dtype)


def _resid_matmul(x, w, layer, h, g, norm_dtype, tm, name):
    lp, k = x.shape
    d = w.shape[2]
    return pl.pallas_call(
        _resid_matmul_kernel,
        grid=(lp // tm,),
        in_specs=[pl.BlockSpec((tm, k), lambda i: (i, 0)),
                  pl.BlockSpec((None, k, d), lambda i: (layer, 0, 0), pipeline_mode=pl.Buffered(1)),
                  pl.BlockSpec((tm, d), lambda i: (i, 0)),
                  pl.BlockSpec((1, d), lambda i: (0, 0))],
        out_specs=[pl.BlockSpec((tm, d), lambda i: (i, 0)),
                   pl.BlockSpec((tm, d), lambda i: (i, 0))],
        out_shape=[jax.ShapeDtypeStruct((lp, d), F32),
                   jax.ShapeDtypeStruct((lp, d), norm_dtype)],
        compiler_params=_params(("parallel",)),
        name=name,
    )(x, w, h, g.reshape(1, d))


def _matmul_resid_kernel(x_ref, w_ref, h_ref, o_ref):
    o_ref[...] = h_ref[...] + _dot(x_ref[...], w_ref[...])


def _matmul_resid(x, w, layer, h, tm, tn, name):
    lp, k = x.shape
    n = w.shape[2]
    return pl.pallas_call(
        _matmul_resid_kernel,
        grid=(lp // tm, n // tn),
        in_specs=[pl.BlockSpec((tm, k), lambda i, j: (i, 0)),
                  pl.BlockSpec((None, k, tn), lambda i, j: (layer, 0, j)),
                  pl.BlockSpec((tm, tn), lambda i, j: (i, j))],
        out_specs=pl.BlockSpec((tm, tn), lambda i, j: (i, j)),
        out_shape=jax.ShapeDtypeStruct((lp, n), F32),
        compiler_params=_params(("parallel", "arbitrary")),
        name=name,
    )(x, w, h)


def _rmsnorm_kernel(h_ref, g_ref, o_ref):
    x = h_ref[...]
    y = x * lax.rsqrt(jnp.mean(x * x, axis=-1, keepdims=True) + EPS)
    o_ref[...] = (y * g_ref[...]).astype(o_ref.dtype)


def _rmsnorm(h, g, tm):
    lp, d = h.shape
    return pl.pallas_call(
        _rmsnorm_kernel,
        grid=(lp // tm,),
        in_specs=[pl.BlockSpec((tm, d), lambda i: (i, 0)),
                  pl.BlockSpec((1, d), lambda i: (0, 0))],
        out_specs=pl.BlockSpec((tm, d), lambda i: (i, 0)),
        out_shape=jax.ShapeDtypeStruct((lp, d), BF16),
        compiler_params=_params(("parallel",)),
        name="rmsnorm",
    )(h, g.reshape(1, d))


def _final_norm_kernel(h_ref, hn_ref, g_ref, o_ref):
    x = jnp.concatenate([h_ref[N_META:, :], hn_ref[...]], axis=0)
    y = x * lax.rsqrt(jnp.mean(x * x, axis=-1, keepdims=True) + EPS)
    o_ref[...] = y * g_ref[...]


def _final_norm(h, g, seq, tm):
    lp, d = h.shape
    mb = tm // N_META
    last = lp // N_META - 1
    return pl.pallas_call(
        _final_norm_kernel,
        grid=(-(-seq // tm),),
        in_specs=[pl.BlockSpec((tm, d), lambda i: (i, 0)),
                  pl.BlockSpec((N_META, d), lambda i: (jnp.minimum((i + 1) * mb, last), 0)),
                  pl.BlockSpec((1, d), lambda i: (0, 0))],
        out_specs=pl.BlockSpec((tm, d), lambda i: (i, 0)),
        out_shape=jax.ShapeDtypeStruct((seq, d), F32),
        compiler_params=_params(("parallel",)),
        name="final_norm",
    )(h, h, g.reshape(1, d))


def _mlp_up_kernel(x_ref, xh_ref, wg_ref, wu_ref, cg_ref, cu_ref, o_ref, *, n_sub):
    rs = x_ref.shape[0] // n_sub
    keep = (pl.program_id(0) > 0).astype(BF16)
    wg = wg_ref[...].astype(BF16)
    wu = wu_ref[...].astype(BF16)
    for s in range(n_sub):
        if s == 0:
            xe = jnp.concatenate([xh_ref[...] * keep, x_ref[:rs, :]], axis=0)
        else:
            xe = x_ref[s * rs - HALO:(s + 1) * rs, :]
        ug = _causal_conv3(_dot(xe, wg), cg_ref)
        uu = _causal_conv3(_dot(xe, wu), cu_ref)
        o_ref[s * rs:(s + 1) * rs, :] = (ug * _sigmoid(ug) * uu).astype(o_ref.dtype)


def _mlp_up(xn, w_up, layer, conv_w, tm, tn):
    lp, d = xn.shape
    nj = D_FF // tn
    hb = tm // HALO
    n_sub = 4 if tm % (4 * HALO) == 0 else 1
    return pl.pallas_call(
        functools.partial(_mlp_up_kernel, n_sub=n_sub),
        grid=(lp // tm, nj),
        in_specs=[pl.BlockSpec((tm, d), lambda i, j: (i, 0)),
                  pl.BlockSpec((HALO, d), lambda i, j: (jnp.maximum(i * hb - 1, 0), 0)),
                  pl.BlockSpec((None, d, tn), lambda i, j: (layer, 0, j)),
                  pl.BlockSpec((None, d, tn), lambda i, j: (layer, 0, nj + j)),
                  pl.BlockSpec((3, tn), lambda i, j: (0, j)),
                  pl.BlockSpec((3, tn), lambda i, j: (0, nj + j))],
        out_specs=pl.BlockSpec((tm, tn), lambda i, j: (i, j)),
        out_shape=jax.ShapeDtypeStruct((lp, D_FF), BF16),
        compiler_params=_params(("parallel", "arbitrary")),
        name="mlp_up",
    )(xn, xn, w_up, w_up, conv_w, conv_w)


def _in_proj_kernel(x_ref, w_ref, s_ref, o_ref):
    w = w_ref[...].astype(BF16)
    o_ref[...] = (_dot_nt(x_ref[...], w) * s_ref[...]).astype(o_ref.dtype)


def _in_proj(xn, w_t, layer, colscale, tm, tn):
    lp, k = xn.shape
    assert COL_QA % tn == 0 and COL_SCB % tn == 0 and N_MAIN % tn == 0
    assert SRC_GATE % 8 == 0 and SRC_MID % 8 == 0 and tn % 8 == 0

    def src_row(j):
        c8 = j * (tn // 8)
        r8 = jnp.where(c8 < COL_QA // 8, SRC_GATE // 8 + c8,
                       jnp.where(c8 < COL_SCB // 8, c8 - COL_QA // 8,
                                 (SRC_MID - COL_SCB) // 8 + c8))
        return r8 * 8

    return pl.pallas_call(
        _in_proj_kernel,
        grid=(lp // tm, N_MAIN // tn),
        in_specs=[pl.BlockSpec((tm, k), lambda i, j: (i, 0)),
                  pl.BlockSpec((None, pl.Element(tn), pl.Element(k)),
                               lambda i, j: (layer, src_row(j), 0)),
                  pl.BlockSpec((1, tn), lambda i, j: (0, j))],
        out_specs=pl.BlockSpec((tm, tn), lambda i, j: (i, j)),
        out_shape=jax.ShapeDtypeStruct((lp, N_MAIN), BF16),
        compiler_params=_params(("parallel", "arbitrary")),
        name="in_proj",
    )(xn, w_t, colscale)


def _in_proj_small_kernel(x_ref, w_ref, o_ref):
    o_ref[...] = _dot_nt(x_ref[...], w_ref[...].astype(BF16))


def _in_proj_small(xn, w_small_t, layer, tm):
    lp, k = xn.shape
    return pl.pallas_call(
        _in_proj_small_kernel,
        grid=(lp // tm,),
        in_specs=[pl.BlockSpec((tm, k), lambda i: (i, 0)),
                  pl.BlockSpec((None, N_SMALL, k), lambda i: (layer, 0, 0))],
        out_specs=pl.BlockSpec((tm, N_SMALL), lambda i: (i, 0)),
        out_shape=jax.ShapeDtypeStruct((lp, N_SMALL), F32),
        compiler_params=_params(("parallel",)),
        name="in_proj_small",
    )(xn, w_small_t)


def _small_w_in(w_t):
    pad = jnp.zeros((w_t.shape[0], N_SMALL - FOX_HEADS - GLA_RANK, w_t.shape[2]), w_t.dtype)
    return jnp.concatenate([w_t[:, SRC_FA:SRC_MID], w_t[:, SRC_GLR:SRC_GATE], pad], axis=1)


def _main_colscale():
    s = jnp.ones((1, N_MAIN), F32)
    s = s.at[:, COL_QA:COL_QA + FOX_WIDTH].set(FOX_DH ** -0.5 * LOG2_E)
    s = s.at[:, COL_QC:COL_QC + GLA_HEADS * GLA_DK].set(GLA_DK ** -0.5)
    return s


def kernel(x, meta_tokens, norm1_g, w_in, fox_b_f, gate_b, conv_w, gla_w_g2, gla_b_g, gla_norm_g,
           w_a_o, w_b_o, w_c_o, w_o, norm2_g, w_up, mlp_conv_w, w_down, final_norm_g):
    assert x.shape[0] == 1 and x.shape[2] == D_MODEL
    depth = w_in.shape[0]
    seq = x.shape[1]
    l_real = N_META + seq
    lp = -(-l_real // ROW_TILE) * ROW_TILE
    tm = ROW_TILE
    blk = ROW_TILE

    h, xn = _embed(x[0], meta_tokens.astype(x.dtype), norm1_g[0], lp, tm)
    colscale = _main_colscale()
    w_t = jnp.swapaxes(w_in, 1, 2)
    w_small_t = _small_w_in(w_t)
    w_a_o, w_b_o, w_c_o, w_o, w_down = (w.astype(BF16) for w in (w_a_o, w_b_o, w_c_o, w_o, w_down))

    for l in range(depth):
        proj = _in_proj(xn, w_t, l, colscale, _row_tile(lp, 1664), 1024)
        small = _in_proj_small(xn, w_small_t, l, tm)

        f_blocks = small[:, :FOX_HEADS].T.reshape(FOX_HEADS, lp // 128, 128).transpose(1, 0, 2)
        c = _fox_cumsum(f_blocks, fox_b_f[l])
        c_blocks = c.transpose(1, 0, 2).reshape(FOX_HEADS, lp // blk, 1, blk)
        oa = _fox_attention(proj, c_blocks, blk, 320, 64, 2)

        wg = gla_w_g2[l].reshape(GLA_RANK, GLA_HEADS, GLA_DK).transpose(1, 0, 2)
        wg_pad = jnp.zeros((GLA_HEADS, N_SMALL, GLA_DK), F32)
        wg_pad = wg_pad.at[:, SMALL_GLR:SMALL_GLR + GLA_RANK, :].set(wg)
        oc = _gla(proj, small, wg_pad, gla_b_g[l].reshape(GLA_HEADS, 1, GLA_DK),
                  gla_norm_g[l].reshape(GLA_HEADS, 1, GLA_DV), ROW_TILE)

        mix = _merge(proj, oa, oc, gate_b[l], conv_w[l], w_a_o, w_b_o, w_c_o, l, 320)
        h, xn2 = _resid_matmul(mix, w_o, l, h, norm2_g[l], BF16, 640, "out_proj")

        act = _mlp_up(xn2, w_up, l, mlp_conv_w[l], _row_tile(lp, 1664), 512)
        h = _matmul_resid(act, w_down, l, h, _row_tile(lp, 832, 64), 512, "mlp_down")
        if l + 1 < depth:
            xn = _rmsnorm(h, norm1_g[l + 1], tm)
    return _final_norm(h, final_norm_g, seq, tm)[None]
```
